```python
import jax, jax.numpy as jnp
from jax import lax
import numpy as np

D_MODEL = 2048
BATCH = 1
SEQ = 8192
DEPTH = 2
DEC_BATCH = 8
DEC_SEQ = 64
PAST_LEN = 2048

CHUNK = 64
POOL_WIDTH = D_MODEL // 2
POOL_WINDOWS = (2, 4, 8, 16)
N_POOL_GROUPS = len(POOL_WINDOWS)
POOL_GROUP = POOL_WIDTH // N_POOL_GROUPS
POOL_HIST = max(POOL_WINDOWS) - 1
CONV_WIDTH = D_MODEL // 2
CONV_K = 31
CONV_HIST = CONV_K - 1
SPLITS = [POOL_WIDTH, 2 * POOL_WIDTH, 2 * POOL_WIDTH + 2 * CONV_WIDTH,
          2 * POOL_WIDTH + 3 * CONV_WIDTH, 2 * POOL_WIDTH + 3 * CONV_WIDTH + D_MODEL]
IN_COLS = 2 * POOL_WIDTH + 3 * CONV_WIDTH + 2 * D_MODEL
RMS_EPS = 1e-6
LN_EPS = 1e-5

kernel_name = "gated_pool_conformer_stream_step"


def rmsnorm(x, g):
    xf = x.astype(jnp.float32)
    y = xf * lax.rsqrt(jnp.mean(xf * xf, axis=-1, keepdims=True) + RMS_EPS)
    return (y * g.astype(jnp.float32)).astype(x.dtype)


def layernorm(x, g, b):
    xf = x.astype(jnp.float32)
    mu = jnp.mean(xf, axis=-1, keepdims=True)
    xc = xf - mu
    var = jnp.mean(xc * xc, axis=-1, keepdims=True)
    y = xc * lax.rsqrt(var + LN_EPS) * g.astype(jnp.float32) + b.astype(jnp.float32)
    return y.astype(x.dtype)


def multiscale_pool(ext_u, start):
    T = ext_u.shape[1] - POOL_HIST
    extf = ext_u.astype(jnp.float32)
    cs = jnp.pad(jnp.cumsum(extf, axis=1), ((0, 0), (1, 0), (0, 0)))
    hi = cs[:, POOL_HIST + 1:, :]
    pos = start + jnp.arange(T)
    outs = []
    for g, w in enumerate(POOL_WINDOWS):
        sl = slice(g * POOL_GROUP, (g + 1) * POOL_GROUP)
        lo = cs[:, POOL_HIST + 1 - w:POOL_HIST + 1 - w + T, sl]
        cnt = jnp.minimum(w, pos + 1).astype(jnp.float32)[None, :, None]
        outs.append((hi[:, :, sl] - lo) / cnt)
    mean = jnp.concatenate(outs, axis=-1)
    return (mean - extf[:, POOL_HIST:, :]).astype(ext_u.dtype)


def causal_depthwise(ext_v, w, b):
    out = lax.conv_general_dilated(
        ext_v, w[:, None, :].astype(ext_v.dtype), window_strides=(1,), padding='VALID',
        dimension_numbers=('NWC', 'WIO', 'NWC'), feature_group_count=CONV_WIDTH)
    return out + b


def mixer_layer(x, hist_pool, hist_conv, start, w_norm, w_in, w_pool_mix, pool_scale,
                w_pool_out, conv_w, conv_b, ln_g, ln_b, w_conv_out, w_out):
    B, T, _ = x.shape
    h = rmsnorm(x, w_norm)
    z = h @ w_in
    u, gp, cv, gc, mp, mc = jnp.split(z, SPLITS, axis=-1)
    ext_u = jnp.concatenate([hist_pool.astype(u.dtype), u], axis=1)
    pooled = multiscale_pool(ext_u, start)
    mixed = jnp.einsum('btgc,gcd->btgd', pooled.reshape(B, T, N_POOL_GROUPS, POOL_GROUP),
                       w_pool_mix).reshape(B, T, POOL_WIDTH) * pool_scale
    pool_branch = (mixed * jax.nn.silu(gp)) @ w_pool_out
    a, bg = jnp.split(cv, 2, axis=-1)
    v = a * jax.nn.sigmoid(bg)
    ext_v = jnp.concatenate([hist_conv.astype(v.dtype), v], axis=1)
    c = jax.nn.silu(layernorm(causal_depthwise(ext_v, conv_w, conv_b), ln_g, ln_b))
    conv_branch = (c * jax.nn.silu(gc)) @ w_conv_out
    merged = jax.nn.sigmoid(mp) * pool_branch + jax.nn.sigmoid(mc) * conv_branch
    y = x + merged @ w_out
    return y, ext_u[:, -POOL_HIST:, :], ext_v[:, -CONV_HIST:, :]


def setup_inputs(seed: int = 0) -> dict:
    key = jax.random.key(seed)
    ks = jax.random.split(key, 16)
    f32 = jnp.float32
    nrm = lambda k, s, sc: jax.random.normal(k, s, f32) * sc
    return {
        "x_prompt": nrm(ks[0], (BATCH, SEQ, D_MODEL), 1.0),
        "x_sample": nrm(ks[1], (DEC_BATCH, DEC_SEQ, D_MODEL), 1.0),
        "cache_pool": nrm(ks[2], (DEPTH, DEC_BATCH, POOL_HIST, POOL_WIDTH), 1.0),
        "cache_conv": nrm(ks[3], (DEPTH, DEC_BATCH, CONV_HIST, CONV_WIDTH), 0.5),
        "w_norm": 1.0 + nrm(ks[4], (DEPTH, D_MODEL), 0.02),
        "w_in": nrm(ks[5], (DEPTH, D_MODEL, IN_COLS), D_MODEL ** -0.5),
        "w_pool_mix": nrm(ks[6], (DEPTH, N_POOL_GROUPS, POOL_GROUP, POOL_GROUP), POOL_GROUP ** -0.5),
        "pool_scale": 1.0 + nrm(ks[7], (DEPTH, POOL_WIDTH), 0.1),
        "w_pool_out": nrm(ks[8], (DEPTH, POOL_WIDTH, D_MODEL), POOL_WIDTH ** -0.5),
        "conv_w": nrm(ks[9], (DEPTH, CONV_K, CONV_WIDTH), CONV_K ** -0.5),
        "conv_b": nrm(ks[10], (DEPTH, CONV_WIDTH), 0.02),
        "ln_g": 1.0 + nrm(ks[11], (DEPTH, CONV_WIDTH), 0.02),
        "ln_b": nrm(ks[12], (DEPTH, CONV_WIDTH), 0.02),
        "w_conv_out": nrm(ks[13], (DEPTH, CONV_WIDTH, D_MODEL), CONV_WIDTH ** -0.5),
        "w_out": nrm(ks[14], (DEPTH, D_MODEL, D_MODEL), D_MODEL ** -0.5),
        "w_final_norm": 1.0 + nrm(ks[15], (D_MODEL,), 0.02),
    }


def reference(x_prompt, x_sample, cache_pool, cache_conv, w_norm, w_in, w_pool_mix, pool_scale,
              w_pool_out, conv_w, conv_b, ln_g, ln_b, w_conv_out, w_out, w_final_norm):
    B = x_prompt.shape[0]
    hp = x_prompt
    hs = x_sample
    pool_p, conv_p, pool_s, conv_s = [], [], [], []
    zero_pool = jnp.zeros((B, POOL_HIST, POOL_WIDTH), x_prompt.dtype)
    zero_conv = jnp.zeros((B, CONV_HIST, CONV_WIDTH), x_prompt.dtype)
    for l in range(DEPTH):
        params = (w_norm[l], w_in[l], w_pool_mix[l], pool_scale[l], w_pool_out[l],
                  conv_w[l], conv_b[l], ln_g[l], ln_b[l], w_conv_out[l], w_out[l])
        hp, sp, sc = mixer_layer(hp, zero_pool, zero_conv, 0, *params)
        hs, tp, tc = mixer_layer(hs, cache_pool[l], cache_conv[l], PAST_LEN, *params)
        pool_p.append(sp)
        conv_p.append(sc)
        pool_s.append(tp)
        conv_s.append(tc)
    y_prompt = rmsnorm(hp, w_final_norm)
    y_sample = rmsnorm(hs, w_final_norm)
    new_pool_state_prompt = jnp.stack(pool_p, axis=0)
    new_conv_state_prompt = jnp.stack(conv_p, axis=0)
    new_pool_state_sample = jnp.stack(pool_s, axis=0)
    new_conv_state_sample = jnp.stack(conv_s, axis=0)
    return (y_prompt, y_sample, new_pool_state_prompt, new_conv_state_prompt,
            new_pool_state_sample, new_conv_state_sample)
```

```python
import functools

import jax
import jax.numpy as jnp
from jax import lax
from jax.experimental import pallas as pl
from jax.experimental.pallas import tpu as pltpu

PAST_LEN = 2048
POOL_WINDOWS = (2, 4, 8, 16)
RMS_EPS = 1e-6
LN_EPS = 1e-5

LANES = 128
SUBLANES = 8
VMEM_LIMIT_BYTES = 58 * 1024 * 1024

BF16 = jnp.bfloat16
F32 = jnp.float32


def _round_up(n, m):
    return (n + m - 1) // m * m


def _sigmoid(x):
    return jax.nn.sigmoid(x)


def _norm_kernel(xp_ref, xs_ref, wn_ref, h_ref, *, n_p):
    i = pl.program_id(0)

    def emit(x_ref):
        x = x_ref[...]
        ms = jnp.mean(x * x, axis=-1, keepdims=True)
        h_ref[...] = (x * lax.rsqrt(ms + RMS_EPS) * wn_ref[...]).astype(BF16)

    @pl.when(i < n_p)
    def _():
        emit(xp_ref)

    @pl.when(i >= n_p)
    def _():
        emit(xs_ref)


def _norm_call(xp, xs, wn, *, tm):
    n_p, n_s = xp.shape[0] // tm, xs.shape[0] // tm
    d = xp.shape[1]
    return pl.pallas_call(
        functools.partial(_norm_kernel, n_p=n_p),
        grid=(n_p + n_s,),
        in_specs=[pl.BlockSpec((tm, d), lambda i: (jnp.minimum(i, n_p - 1), 0)),
                  pl.BlockSpec((tm, d), lambda i: (jnp.maximum(i - n_p, 0), 0)),
                  pl.BlockSpec((1, d), lambda i: (0, 0))],
        out_specs=pl.BlockSpec((tm, d), lambda i: (i, 0)),
        out_shape=jax.ShapeDtypeStruct((xp.shape[0] + xs.shape[0], d), BF16),
        compiler_params=pltpu.CompilerParams(dimension_semantics=("arbitrary",)),
        name="rmsnorm_in",
    )(xp, xs, wn)


def _proj_kernel(h_ref, wa_ref, wb_ref, uv_ref, g_ref, wa_bf, wb_bf, *, nb):
    j = pl.program_id(0)
    i = pl.program_id(1)
    is_u = j < nb
    is_v = (j >= nb) & (j < 2 * nb)
    is_silu = (j >= 2 * nb) & (j < 4 * nb)
    is_sig = j >= 4 * nb

    @pl.when(i == 0)
    def _():
        wa_bf[...] = wa_ref[...].astype(BF16)

    @pl.when((i == 0) & is_v)
    def _():
        wb_bf[...] = wb_ref[...].astype(BF16)

    def proj(w_bf):
        return jnp.dot(h_ref[...], w_bf[...], preferred_element_type=F32)

    @pl.when(is_u)
    def _():
        uv_ref[...] = proj(wa_bf)

    @pl.when(is_v)
    def _():
        uv_ref[...] = proj(wa_bf) * _sigmoid(proj(wb_bf))

    @pl.when(is_silu)
    def _():
        z = proj(wa_bf)
        g_ref[...] = (z * _sigmoid(z)).astype(BF16)

    @pl.when(is_sig)
    def _():
        g_ref[...] = _sigmoid(proj(wa_bf)).astype(BF16)


def _proj_call(h, w_in, *, tm, tn, pw):
    n, d = h.shape
    nb = pw // tn
    n_i = n // tm
    n_j = (2 * pw + 2 * pw + 2 * d) // tn

    def wa_map(j, i):
        blk = jnp.where(j < nb, j, jnp.where(j < 2 * nb, j + nb, jnp.where(j < 3 * nb, j - nb, j + nb)))
        return (0, blk)

    def wb_map(j, i):
        return (0, jnp.clip(j + 2 * nb, 3 * nb, 4 * nb - 1))

    def uv_map(j, i):
        live = j < 2 * nb
        return (jnp.where(live, i, n_i - 1), jnp.where(live, j, 2 * nb - 1))

    def g_map(j, i):
        live = j >= 2 * nb
        return (jnp.where(live, i, 0), jnp.where(live, j - 2 * nb, 0))

    return pl.pallas_call(
        functools.partial(_proj_kernel, nb=nb),
        grid=(n_j, n_i),
        in_specs=[pl.BlockSpec((tm, d), lambda j, i: (i, 0)),
                  pl.BlockSpec((d, tn), wa_map),
                  pl.BlockSpec((d, tn), wb_map)],
        out_specs=[pl.BlockSpec((tm, tn), uv_map),
                   pl.BlockSpec((tm, tn), g_map)],
        out_shape=[jax.ShapeDtypeStruct((n, 2 * pw), F32),
                   jax.ShapeDtypeStruct((n, 2 * pw + 2 * d), BF16)],
        scratch_shapes=[pltpu.VMEM((d, tn), BF16), pltpu.VMEM((d, tn), BF16)],
        compiler_params=pltpu.CompilerParams(dimension_semantics=("arbitrary", "arbitrary"),
                                             vmem_limit_bytes=VMEM_LIMIT_BYTES),
        name="in_proj",
    )(h, w_in, w_in)


def _mixer_kernel(*refs, tm, n_p, seg, split_x, last, conv_k, hpu, hpv):
    n_x = 2 if split_x else 1
    u_ref, v_ref, sgp_ref, sgc_ref, smp_ref, smc_ref = refs[:6]
    x_refs = refs[6:6 + n_x]
    (cpool_ref, cconv_ref, wmix_ref, wpo_ref, wco_ref, wout_ref, pscale_ref, cw_ref, cb_ref,
     lng_ref, lnb_ref, wn_ref) = refs[6 + n_x:18 + n_x]
    out_refs = refs[18 + n_x:20 + n_x]
    extu, extv, pooled_scr, conv_scr = refs[20 + n_x:]
    n_chunks = u_ref.shape[1] // LANES
    grp_chunks = n_chunks // len(POOL_WINDOWS)
    i = pl.program_id(0)

    def pool_rows(base, nrows, pos0, out_row0):
        for g, w in enumerate(POOL_WINDOWS):
            for cc in range(grp_chunks):
                c = g * grp_chunks + cc
                cur = extu[c, base + hpu:base + hpu + nrows, :]
                s = cur
                for o in range(1, w):
                    s = s + extu[c, base + hpu - o:base + hpu - o + nrows, :]
                if pos0 is None:
                    mean = s * (1.0 / w)
                else:
                    pos = pos0 + lax.broadcasted_iota(jnp.int32, (nrows, LANES), 0)
                    mean = s / jnp.minimum(w, pos + 1).astype(F32)
                pooled_scr[out_row0:out_row0 + nrows, LANES * c:LANES * (c + 1)] = mean - cur

    def conv_rows(base, nrows, out_row0):
        rb = min(64, nrows)
        off = base + hpv - (conv_k - 1)
        for c in range(n_chunks):
            cols = slice(LANES * c, LANES * (c + 1))
            wv = [jnp.broadcast_to(cw_ref[k:k + 1, cols], (rb, LANES)) for k in range(conv_k)]
            bv = jnp.broadcast_to(cb_ref[0:1, cols], (rb, LANES))

            def block(r):
                acc = bv
                for k in range(conv_k):
                    acc = acc + wv[k] * extv[c, pl.ds(off + k + r, rb), :]
                conv_scr[pl.ds(out_row0 + r, rb), cols] = acc

            def body(t, carry):
                block(pl.multiple_of(t * rb, rb))
                return carry

            if nrows == rb:
                block(0)
            else:
                lax.fori_loop(0, nrows // rb, body, 0)

    def tail(x_ref, o_refs):
        pooled = pooled_scr[...]
        gw = pooled.shape[1] // len(POOL_WINDOWS)
        mixed = jnp.concatenate(
            [jnp.dot(pooled[:, gw * g:gw * (g + 1)].astype(BF16), wmix_ref[g], preferred_element_type=F32)
             for g in range(len(POOL_WINDOWS))], axis=1)
        pool_in = (mixed * pscale_ref[...] * sgp_ref[...].astype(F32)).astype(BF16)
        pool_branch = jnp.dot(pool_in, wpo_ref[...], preferred_element_type=F32)

        cv = conv_scr[...]
        mu = jnp.mean(cv, axis=-1, keepdims=True)
        xc = cv - mu
        var = jnp.mean(xc * xc, axis=-1, keepdims=True)
        ln = xc * lax.rsqrt(var + LN_EPS) * lng_ref[...] + lnb_ref[...]
        conv_in = (ln * _sigmoid(ln) * sgc_ref[...].astype(F32)).astype(BF16)
        conv_branch = jnp.dot(conv_in, wco_ref[...], preferred_element_type=F32)

        merged = (smp_ref[...].astype(F32) * pool_branch + smc_ref[...].astype(F32) * conv_branch).astype(BF16)
        y = x_ref[...] + jnp.dot(merged, wout_ref[...], preferred_element_type=F32)
        ms = jnp.mean(y * y, axis=-1, keepdims=True)
        hn = y * lax.rsqrt(ms + RMS_EPS) * wn_ref[...]
        if last:
            o_refs[0][...] = hn
        else:
            o_refs[0][...] = y
            o_refs[1][...] = hn.astype(BF16)

    @pl.when(i < n_p)
    def _():
        @pl.when(i == 0)
        def _():
            extu[:, 0:hpu, :] = jnp.zeros((n_chunks, hpu, LANES), F32)
            extv[:, 0:hpv, :] = jnp.zeros((n_chunks, hpv, LANES), F32)

        @pl.when(i > 0)
        def _():
            extu[:, 0:hpu, :] = extu[:, tm:tm + hpu, :]
            extv[:, 0:hpv, :] = extv[:, tm:tm + hpv, :]

        for c in range(n_chunks):
            cols = slice(LANES * c, LANES * (c + 1))
            extu[c, hpu:hpu + tm, :] = u_ref[:, cols]
            extv[c, hpv:hpv + tm, :] = v_ref[:, cols]
        pool_rows(0, tm, i * tm, 0)
        conv_rows(0, tm, 0)
        tail(x_refs[0], out_refs if not last else out_refs[0:1])

    @pl.when(i >= n_p)
    def _():
        spt = tm // seg
        for s in range(spt):
            q = (i - n_p) * spt + s
            bu, bv = s * (hpu + seg), s * (hpv + seg)
            rows = slice(seg * s, seg * (s + 1))
            for c in range(n_chunks):
                cols = slice(LANES * c, LANES * (c + 1))
                extu[c, bu:bu + hpu, :] = cpool_ref[q, :, cols]
                extu[c, bu + hpu:bu + hpu + seg, :] = u_ref[rows, cols]
                extv[c, bv:bv + hpv, :] = cconv_ref[q, :, cols]
                extv[c, bv + hpv:bv + hpv + seg, :] = v_ref[rows, cols]
            pool_rows(bu, seg, None, seg * s)
            conv_rows(bv, seg, seg * s)
        tail(x_refs[-1], out_refs if not last else out_refs[1:2])


def _mixer_call(uv, gates, xs, cpool, cconv, wmix, wpo, wco, wout, pscale, cw, cb, lng, lnb, wn,
                *, tm, n_prompt_rows, seg, last):
    n = uv.shape[0]
    pw = uv.shape[1] // 2
    d = wout.shape[1]
    n_p = n_prompt_rows // tm
    n_s = (n - n_prompt_rows) // tm
    split_x = len(xs) == 2
    conv_k = cw.shape[0]
    hpu, hpv = cpool.shape[1], cconv.shape[1]
    spt = tm // seg
    assert PAST_LEN >= max(POOL_WINDOWS) - 1 and seg >= hpv and n_prompt_rows % tm == 0 and tm % seg == 0

    def const(shape):
        return pl.BlockSpec(shape, lambda i: (0,) * len(shape), pipeline_mode=pl.Buffered(1))

    row = lambda c: (lambda i: (i, c))
    if split_x:
        x_specs = [pl.BlockSpec((tm, d), lambda i: (jnp.minimum(i, n_p - 1), 0)),
                   pl.BlockSpec((tm, d), lambda i: (jnp.maximum(i - n_p, 0), 0))]
    else:
        x_specs = [pl.BlockSpec((tm, d), row(0))]
    in_specs = ([pl.BlockSpec((tm, pw), row(0)), pl.BlockSpec((tm, pw), row(1)),
                 pl.BlockSpec((tm, pw), row(0)), pl.BlockSpec((tm, pw), row(1)),
                 pl.BlockSpec((tm, d), row(1)), pl.BlockSpec((tm, d), row(2))]
                + x_specs
                + [const(cpool.shape), const(cconv.shape), const(wmix.shape), const(wpo.shape),
                   const(wco.shape), const(wout.shape), const(pscale.shape), const(cw.shape),
                   const(cb.shape), const(lng.shape), const(lnb.shape), const(wn.shape)])
    if last:
        out_specs = [pl.BlockSpec((tm, d), lambda i: (jnp.minimum(i, n_p - 1), 0)),
                     pl.BlockSpec((tm, d), lambda i: (jnp.maximum(i - n_p, 0), 0))]
        out_shape = [jax.ShapeDtypeStruct((n_prompt_rows, d), F32),
                     jax.ShapeDtypeStruct((n - n_prompt_rows, d), F32)]
    else:
        out_specs = [pl.BlockSpec((tm, d), row(0)), pl.BlockSpec((tm, d), row(0))]
        out_shape = [jax.ShapeDtypeStruct((n, d), F32), jax.ShapeDtypeStruct((n, d), BF16)]
    n_chunks = pw // LANES
    rows_u = max(hpu + tm, spt * (hpu + seg))
    rows_v = max(hpv + tm, spt * (hpv + seg))
    kern = functools.partial(_mixer_kernel, tm=tm, n_p=n_p, seg=seg, split_x=split_x, last=last,
                             conv_k=conv_k, hpu=hpu, hpv=hpv)
    return pl.pallas_call(
        kern,
        grid=(n_p + n_s,),
        in_specs=in_specs,
        out_specs=out_specs,
        out_shape=out_shape,
        scratch_shapes=[pltpu.VMEM((n_chunks, rows_u, LANES), F32),
                        pltpu.VMEM((n_chunks, rows_v, LANES), F32),
                        pltpu.VMEM((tm, pw), F32),
                        pltpu.VMEM((tm, pw), F32)],
        compiler_params=pltpu.CompilerParams(dimension_semantics=("arbitrary",),
                                             vmem_limit_bytes=VMEM_LIMIT_BYTES),
        name="mixer_last" if last else "mixer",
    )(uv, uv, gates, gates, gates, gates, *xs, cpool, cconv, wmix, wpo, wco, wout, pscale, cw, cb,
      lng, lnb, wn)


def kernel(x_prompt, x_sample, cache_pool, cache_conv, w_norm, w_in, w_pool_mix, pool_scale, w_pool_out,
           conv_w, conv_b, ln_g, ln_b, w_conv_out, w_out, w_final_norm):
    b, t, d = x_prompt.shape
    sb, st, _ = x_sample.shape
    depth = w_norm.shape[0]
    pw = pool_scale.shape[1]
    hu, hv = cache_pool.shape[2], cache_conv.shape[2]
    assert b == 1 and conv_w.shape[2] == pw and 2 * pw == d
    assert max(POOL_WINDOWS) - 1 == hu and conv_w.shape[1] - 1 == hv
    n_prompt = b * t
    xp = x_prompt.reshape(n_prompt, d)
    xs = x_sample.reshape(sb * st, d)
    hpu, hpv = _round_up(hu, SUBLANES), _round_up(hv, SUBLANES)
    cpool = jnp.pad(cache_pool, ((0, 0), (0, 0), (hpu - hu, 0), (0, 0)))
    cconv = jnp.pad(cache_conv, ((0, 0), (0, 0), (hpv - hv, 0), (0, 0)))
    row2 = lambda a: a.reshape(depth, 1, a.shape[-1])
    pscale, cb, lng, lnb = row2(pool_scale), row2(conv_b), row2(ln_g), row2(ln_b)
    wnext = jnp.concatenate([w_norm[1:], w_final_norm[None]], axis=0).reshape(depth, 1, d)

    h = _norm_call(xp, xs, w_norm[0:1], tm=512)
    x_parts = (xp, xs)
    pool_p, conv_p, pool_s, conv_s = [], [], [], []
    outs = None
    for l in range(depth):
        uv, gates = _proj_call(h, w_in[l], tm=512, tn=512, pw=pw)
        last = l == depth - 1
        outs = _mixer_call(uv, gates, x_parts, cpool[l], cconv[l],
                           w_pool_mix[l].astype(BF16), w_pool_out[l].astype(BF16),
                           w_conv_out[l].astype(BF16), w_out[l].astype(BF16),
                           pscale[l], conv_w[l], cb[l], lng[l], lnb[l], wnext[l],
                           tm=256, n_prompt_rows=n_prompt, seg=st, last=last)
        if not last:
            x_parts = (outs[0],)
            h = outs[1]
        uvs = uv[n_prompt:].reshape(sb, st, 2 * pw)
        pool_p.append(uv[n_prompt - hu:n_prompt, :pw].reshape(b, hu, pw))
        conv_p.append(uv[n_prompt - hv:n_prompt, pw:].reshape(b, hv, pw))
        pool_s.append(uvs[:, st - hu:, :pw])
        conv_s.append(uvs[:, st - hv:, pw:])
    y_prompt = outs[0].reshape(b, t, d)
    y_sample = outs[1].reshape(sb, st, d)
    return (y_prompt, y_sample, jnp.stack(pool_p), jnp.stack(conv_p), jnp.stack(pool_s), jnp.stack(conv_s))
```

```python
import functools

import jax
import jax.numpy as jnp
from jax import lax
from jax.experimental import pallas as pl
from jax.experimental.pallas import tpu as pltpu

PAST_LEN = 2048
POOL_WINDOWS = (2, 4, 8, 16)
RMS_EPS = 1e-6
LN_EPS = 1e-5

LANES = 128
SUBLANES = 8
VMEM_LIMIT_BYTES = 58 * 1024 * 1024

NORM_TM = 512
PROJ_TM = 544
PROJ_TN = 1024
MIXER_TM = 256

BF16 = jnp.bfloat16
F32 = jnp.float32


def _round_up(n, m):
    return (n + m - 1) // m * m


def _sigmoid(x):
    return jax.nn.sigmoid(x)


def _norm_kernel(xp_ref, xs_ref, wn_ref, h_ref, *, n_p):
    i = pl.program_id(0)

    def emit(x_ref):
        x = x_ref[...]
        ms = jnp.mean(x * x, axis=-1, keepdims=True)
        h_ref[...] = (x * lax.rsqrt(ms + RMS_EPS) * wn_ref[...]).astype(BF16)

    @pl.when(i < n_p)
    def _():
        emit(xp_ref)

    @pl.when(i >= n_p)
    def _():
        emit(xs_ref)


def _norm_call(xp, xs, wn, *, tm):
    n_p, n_s = xp.shape[0] // tm, xs.shape[0] // tm
    d = xp.shape[1]
    return pl.pallas_call(
        functools.partial(_norm_kernel, n_p=n_p),
        grid=(n_p + n_s,),
        in_specs=[pl.BlockSpec((tm, d), lambda i: (jnp.minimum(i, n_p - 1), 0)),
                  pl.BlockSpec((tm, d), lambda i: (jnp.maximum(i - n_p, 0), 0)),
                  pl.BlockSpec((1, d), lambda i: (0, 0))],
        out_specs=pl.BlockSpec((tm, d), lambda i: (i, 0)),
        out_shape=jax.ShapeDtypeStruct((xp.shape[0] + xs.shape[0], d), BF16),
        compiler_params=pltpu.CompilerParams(dimension_semantics=("arbitrary",)),
        name="rmsnorm_in",
    )(xp, xs, wn)


def _proj_kernel(h_ref, wa_ref, wb_ref, uv_ref, g_ref, wa_bf, wb_bf, *, nb):
    j = pl.program_id(0)
    i = pl.program_id(1)
    is_u = j < nb
    is_v = (j >= nb) & (j < 2 * nb)
    is_silu = (j >= 2 * nb) & (j < 4 * nb)
    is_sig = j >= 4 * nb

    @pl.when(i == 0)
    def _():
        wa_bf[...] = wa_ref[...].astype(BF16)

    @pl.when((i == 0) & is_v)
    def _():
        wb_bf[...] = wb_ref[...].astype(BF16)

    def proj(w_bf):
        return jnp.dot(h_ref[...], w_bf[...], preferred_element_type=F32)

    @pl.when(is_u)
    def _():
        uv_ref[...] = proj(wa_bf)

    @pl.when(is_v)
    def _():
        uv_ref[...] = proj(wa_bf) * _sigmoid(proj(wb_bf))

    @pl.when(is_silu)
    def _():
        z = proj(wa_bf)
        g_ref[...] = (z * _sigmoid(z)).astype(BF16)

    @pl.when(is_sig)
    def _():
        g_ref[...] = _sigmoid(proj(wa_bf)).astype(BF16)


def _proj_call(h, w_in, layer, *, tm, tn, pw):
    n, d = h.shape
    nb = pw // tn
    n_i = n // tm
    n_j = (2 * pw + 2 * pw + 2 * d) // tn

    def wa_map(j, i):
        blk = jnp.where(j < nb, j, jnp.where(j < 2 * nb, j + nb, jnp.where(j < 3 * nb, j - nb, j + nb)))
        return (layer, 0, blk)

    def wb_map(j, i):
        return (layer, 0, jnp.clip(j + 2 * nb, 3 * nb, 4 * nb - 1))

    wb_mode = dict(pipeline_mode=pl.Buffered(1)) if nb == 1 else {}

    def uv_map(j, i):
        live = j < 2 * nb
        return (jnp.where(live, i, n_i - 1), jnp.where(live, j, 2 * nb - 1))

    def g_map(j, i):
        live = j >= 2 * nb
        return (jnp.where(live, i, 0), jnp.where(live, j - 2 * nb, 0))

    return pl.pallas_call(
        functools.partial(_proj_kernel, nb=nb),
        grid=(n_j, n_i),
        in_specs=[pl.BlockSpec((tm, d), lambda j, i: (i, 0)),
                  pl.BlockSpec((None, d, tn), wa_map),
                  pl.BlockSpec((None, d, tn), wb_map, **wb_mode)],
        out_specs=[pl.BlockSpec((tm, tn), uv_map),
                   pl.BlockSpec((tm, tn), g_map)],
        out_shape=[jax.ShapeDtypeStruct((n, 2 * pw), F32),
                   jax.ShapeDtypeStruct((n, 2 * pw + 2 * d), BF16)],
        scratch_shapes=[pltpu.VMEM((d, tn), BF16), pltpu.VMEM((d, tn), BF16)],
        compiler_params=pltpu.CompilerParams(dimension_semantics=("arbitrary", "arbitrary"),
                                             vmem_limit_bytes=VMEM_LIMIT_BYTES),
        name="in_proj",
    )(h, w_in, w_in)


def _mixer_kernel(*refs, tm, n_p, seg, split_x, last, conv_k, hpu, hpv):
    n_x = 2 if split_x else 1
    u_ref, v_ref, sgp_ref, sgc_ref, smp_ref, smc_ref = refs[:6]
    x_refs = refs[6:6 + n_x]
    (cpool_ref, cconv_ref, wmix_ref, wpo_ref, wco_ref, wout_ref, pscale_ref, cw_ref, cb_ref,
     lng_ref, lnb_ref, wn_ref) = refs[6 + n_x:18 + n_x]
    out_refs = refs[18 + n_x:20 + n_x]
    extu, extv, pooled_scr, conv_scr = refs[20 + n_x:]
    n_chunks = u_ref.shape[1] // LANES
    grp_chunks = n_chunks // len(POOL_WINDOWS)
    i = pl.program_id(0)

    def pool_rows(base, nrows, pos0, out_row0):
        for g, w in enumerate(POOL_WINDOWS):
            for cc in range(grp_chunks):
                c = g * grp_chunks + cc
                cur = extu[c, base + hpu:base + hpu + nrows, :]
                s = cur
                for o in range(1, w):
                    s = s + extu[c, base + hpu - o:base + hpu - o + nrows, :]
                if pos0 is None:
                    mean = s * (1.0 / w)
                else:
                    pos = pos0 + lax.broadcasted_iota(jnp.int32, (nrows, LANES), 0)
                    mean = s / jnp.minimum(w, pos + 1).astype(F32)
                pooled_scr[out_row0:out_row0 + nrows, LANES * c:LANES * (c + 1)] = mean - cur

    def conv_rows(base, nrows, out_row0):
        rb = min(64, nrows)
        off = base + hpv - (conv_k - 1)
        for c in range(n_chunks):
            cols = slice(LANES * c, LANES * (c + 1))
            wv = [jnp.broadcast_to(cw_ref[k:k + 1, cols], (rb, LANES)) for k in range(conv_k)]
            bv = jnp.broadcast_to(cb_ref[0:1, cols], (rb, LANES))

            def block(r):
                acc = bv
                for k in range(conv_k):
                    acc = acc + wv[k] * extv[c, pl.ds(off + k + r, rb), :]
                conv_scr[pl.ds(out_row0 + r, rb), cols] = acc

            def body(t, carry):
                block(pl.multiple_of(t * rb, rb))
                return carry

            if nrows == rb:
                block(0)
            else:
                lax.fori_loop(0, nrows // rb, body, 0)

    def tail(x_ref, o_refs):
        pooled = pooled_scr[...]
        gw = pooled.shape[1] // len(POOL_WINDOWS)
        mixed = jnp.concatenate(
            [jnp.dot(pooled[:, gw * g:gw * (g + 1)].astype(BF16), wmix_ref[g], preferred_element_type=F32)
             for g in range(len(POOL_WINDOWS))], axis=1)
        pool_in = (mixed * pscale_ref[...] * sgp_ref[...].astype(F32)).astype(BF16)
        pool_branch = jnp.dot(pool_in, wpo_ref[...], preferred_element_type=F32)

        cv = conv_scr[...]
        mu = jnp.mean(cv, axis=-1, keepdims=True)
        xc = cv - mu
        var = jnp.mean(xc * xc, axis=-1, keepdims=True)
        ln = xc * lax.rsqrt(var + LN_EPS) * lng_ref[...] + lnb_ref[...]
        conv_in = (ln * _sigmoid(ln) * sgc_ref[...].astype(F32)).astype(BF16)
        conv_branch = jnp.dot(conv_in, wco_ref[...], preferred_element_type=F32)

        merged = (smp_ref[...].astype(F32) * pool_branch + smc_ref[...].astype(F32) * conv_branch).astype(BF16)
        y = x_ref[...] + jnp.dot(merged, wout_ref[...], preferred_element_type=F32)
        ms = jnp.mean(y * y, axis=-1, keepdims=True)
        hn = y * lax.rsqrt(ms + RMS_EPS) * wn_ref[...]
        if last:
            o_refs[0][...] = hn
        else:
            o_refs[0][...] = y
            o_refs[1][...] = hn.astype(BF16)

    @pl.when(i < n_p)
    def _():
        @pl.when(i == 0)
        def _():
            extu[:, 0:hpu, :] = jnp.zeros((n_chunks, hpu, LANES), F32)
            extv[:, 0:hpv, :] = jnp.zeros((n_chunks, hpv, LANES), F32)

        @pl.when(i > 0)
        def _():
            extu[:, 0:hpu, :] = extu[:, tm:tm + hpu, :]
            extv[:, 0:hpv, :] = extv[:, tm:tm + hpv, :]

        for c in range(n_chunks):
            cols = slice(LANES * c, LANES * (c + 1))
            extu[c, hpu:hpu + tm, :] = u_ref[:, cols]
            extv[c, hpv:hpv + tm, :] = v_ref[:, cols]
        pool_rows(0, tm, i * tm, 0)
        conv_rows(0, tm, 0)
        tail(x_refs[0], out_refs if not last else out_refs[0:1])

    @pl.when(i >= n_p)
    def _():
        spt = tm // seg
        for s in range(spt):
            q = (i - n_p) * spt + s
            bu, bv = s * (hpu + seg), s * (hpv + seg)
            rows = slice(seg * s, seg * (s + 1))
            for c in range(n_chunks):
                cols = slice(LANES * c, LANES * (c + 1))
                extu[c, bu:bu + hpu, :] = cpool_ref[q, :, cols]
                extu[c, bu + hpu:bu + hpu + seg, :] = u_ref[rows, cols]
                extv[c, bv:bv + hpv, :] = cconv_ref[q, :, cols]
                extv[c, bv + hpv:bv + hpv + seg, :] = v_ref[rows, cols]
            pool_rows(bu, seg, None, seg * s)
            conv_rows(bv, seg, seg * s)
        tail(x_refs[-1], out_refs if not last else out_refs[1:2])


def _mixer_call(uv, gates, xs, layer, cpool, cconv, wmix, wpo, wco, wout, pscale, cw, cb, lng, lnb, wn,
                *, tm, n_prompt_rows, seg, last):
    n = uv.shape[0]
    pw = uv.shape[1] // 2
    d = wout.shape[2]
    n_p = n_prompt_rows // tm
    n_s = (n - n_prompt_rows) // tm
    split_x = len(xs) == 2
    conv_k = cw.shape[1]
    hpu, hpv = cpool.shape[2], cconv.shape[2]
    spt = tm // seg
    assert PAST_LEN >= max(POOL_WINDOWS) - 1 and seg >= hpv and n_prompt_rows % tm == 0 and tm % seg == 0

    def const(a):
        rest = a.shape[1:]
        return pl.BlockSpec((None,) + rest, lambda i: (layer,) + (0,) * len(rest),
                            pipeline_mode=pl.Buffered(1))

    row = lambda c: (lambda i: (i, c))
    if split_x:
        x_specs = [pl.BlockSpec((tm, d), lambda i: (jnp.minimum(i, n_p - 1), 0)),
                   pl.BlockSpec((tm, d), lambda i: (jnp.maximum(i - n_p, 0), 0))]
    else:
        x_specs = [pl.BlockSpec((tm, d), row(0))]
    in_specs = ([pl.BlockSpec((tm, pw), row(0)), pl.BlockSpec((tm, pw), row(1)),
                 pl.BlockSpec((tm, pw), row(0)), pl.BlockSpec((tm, pw), row(1)),
                 pl.BlockSpec((tm, d), row(1)), pl.BlockSpec((tm, d), row(2))]
                + x_specs
                + [const(a) for a in (cpool, cconv, wmix, wpo, wco, wout, pscale, cw, cb, lng, lnb, wn)])
    if last:
        out_specs = [pl.BlockSpec((tm, d), lambda i: (jnp.minimum(i, n_p - 1), 0)),
                     pl.BlockSpec((tm, d), lambda i: (jnp.maximum(i - n_p, 0), 0))]
        out_shape = [jax.ShapeDtypeStruct((n_prompt_rows, d), F32),
                     jax.ShapeDtypeStruct((n - n_prompt_rows, d), F32)]
    else:
        out_specs = [pl.BlockSpec((tm, d), row(0)), pl.BlockSpec((tm, d), row(0))]
        out_shape = [jax.ShapeDtypeStruct((n, d), F32), jax.ShapeDtypeStruct((n, d), BF16)]
    n_chunks = pw // LANES
    rows_u = max(hpu + tm, spt * (hpu + seg))
    rows_v = max(hpv + tm, spt * (hpv + seg))
    kern = functools.partial(_mixer_kernel, tm=tm, n_p=n_p, seg=seg, split_x=split_x, last=last,
                             conv_k=conv_k, hpu=hpu, hpv=hpv)
    return pl.pallas_call(
        kern,
        grid=(n_p + n_s,),
        in_specs=in_specs,
        out_specs=out_specs,
        out_shape=out_shape,
        scratch_shapes=[pltpu.VMEM((n_chunks, rows_u, LANES), F32),
                        pltpu.VMEM((n_chunks, rows_v, LANES), F32),
                        pltpu.VMEM((tm, pw), F32),
                        pltpu.VMEM((tm, pw), F32)],
        compiler_params=pltpu.CompilerParams(dimension_semantics=("arbitrary",),
                                             vmem_limit_bytes=VMEM_LIMIT_BYTES),
        name="mixer_last" if last else "mixer",
    )(uv, uv, gates, gates, gates, gates, *xs, cpool, cconv, wmix, wpo, wco, wout, pscale, cw, cb,
      lng, lnb, wn)


def kernel(x_prompt, x_sample, cache_pool, cache_conv, w_norm, w_in, w_pool_mix, pool_scale, w_pool_out,
           conv_w, conv_b, ln_g, ln_b, w_conv_out, w_out, w_final_norm):
    b, t, d = x_prompt.shape
    sb, st, _ = x_sample.shape
    depth = w_norm.shape[0]
    pw = pool_scale.shape[1]
    hu, hv = cache_pool.shape[2], cache_conv.shape[2]
    assert b == 1 and conv_w.shape[2] == pw and 2 * pw == d
    assert max(POOL_WINDOWS) - 1 == hu and conv_w.shape[1] - 1 == hv
    n_prompt = b * t
    xp = x_prompt.reshape(n_prompt, d)
    xs = x_sample.reshape(sb * st, d)
    hpu, hpv = _round_up(hu, SUBLANES), _round_up(hv, SUBLANES)
    cpool = jnp.pad(cache_pool, ((0, 0), (0, 0), (hpu - hu, 0), (0, 0)))
    cconv = jnp.pad(cache_conv, ((0, 0), (0, 0), (hpv - hv, 0), (0, 0)))
    row2 = lambda a: a.reshape(depth, 1, a.shape[-1])
    pscale, cb, lng, lnb = row2(pool_scale), row2(conv_b), row2(ln_g), row2(ln_b)
    wnext = jnp.concatenate([w_norm[1:], w_final_norm[None]], axis=0).reshape(depth, 1, d)

    wmix, wpo, wco, wout = (w.astype(BF16) for w in (w_pool_mix, w_pool_out, w_conv_out, w_out))

    h = _norm_call(xp, xs, w_norm[0:1], tm=NORM_TM)
    x_parts = (xp, xs)
    pool_p, conv_p, pool_s, conv_s = [], [], [], []
    outs = None
    for l in range(depth):
        uv, gates = _proj_call(h, w_in, l, tm=PROJ_TM, tn=PROJ_TN, pw=pw)
        last = l == depth - 1
        outs = _mixer_call(uv, gates, x_parts, l, cpool, cconv, wmix, wpo, wco, wout,
                           pscale, conv_w, cb, lng, lnb, wnext,
                           tm=MIXER_TM, n_prompt_rows=n_prompt, seg=st, last=last)
        if not last:
            x_parts = (outs[0],)
            h = outs[1]
        uvs = uv[n_prompt:].reshape(sb, st, 2 * pw)
        pool_p.append(uv[n_prompt - hu:n_prompt, :pw].reshape(b, hu, pw))
        conv_p.append(uv[n_prompt - hv:n_prompt, pw:].reshape(b, hv, pw))
        pool_s.append(uvs[:, st - hu:, :pw])
        conv_s.append(uvs[:, st - hv:, pw:])
    y_prompt = outs[0].reshape(b, t, d)
    y_sample = outs[1].reshape(sb, st, d)
    return (y_prompt, y_sample, jnp.stack(pool_p), jnp.stack(conv_p), jnp.stack(pool_s), jnp.stack(conv_s))
```

```python
import functools

import jax
import jax.numpy as jnp
from jax import lax
from jax.experimental import pallas as pl
from jax.experimental.pallas import tpu as pltpu

PAST_LEN = 2048
POOL_WINDOWS = (2, 4, 8, 16)
RMS_EPS = 1e-6
LN_EPS = 1e-5

LANES = 128
SUBLANES = 8
VMEM_LIMIT_BYTES = 58 * 1024 * 1024

NORM_TM = 512
PROJ_UV_TM = 544
PROJ_GATE_TM = 512
PROJ_TN = 1024
CONV_CHUNKS_PER_JOB = 2
CONV_ROWS = SUBLANES
MIXER_TM = 256

BF16 = jnp.bfloat16
F32 = jnp.float32


def _round_up(n, m):
    return (n + m - 1) // m * m


def _sigmoid(x):
    return jax.nn.sigmoid(x)


def _norm_kernel(xp_ref, xs_ref, wn_ref, h_ref, *, n_p):
    i = pl.program_id(0)

    def emit(x_ref):
        x = x_ref[...]
        ms = jnp.mean(x * x, axis=-1, keepdims=True)
        h_ref[...] = (x * lax.rsqrt(ms + RMS_EPS) * wn_ref[...]).astype(BF16)

    @pl.when(i < n_p)
    def _():
        emit(xp_ref)

    @pl.when(i >= n_p)
    def _():
        emit(xs_ref)


def _norm_call(xp, xs, wn, *, tm):
    n_p, n_s = xp.shape[0] // tm, xs.shape[0] // tm
    d = xp.shape[1]
    return pl.pallas_call(
        functools.partial(_norm_kernel, n_p=n_p),
        grid=(n_p + n_s,),
        in_specs=[pl.BlockSpec((tm, d), lambda i: (jnp.minimum(i, n_p - 1), 0)),
                  pl.BlockSpec((tm, d), lambda i: (jnp.maximum(i - n_p, 0), 0)),
                  pl.BlockSpec((1, d), lambda i: (0, 0))],
        out_specs=pl.BlockSpec((tm, d), lambda i: (i, 0)),
        out_shape=jax.ShapeDtypeStruct((xp.shape[0] + xs.shape[0], d), BF16),
        compiler_params=pltpu.CompilerParams(dimension_semantics=("arbitrary",)),
        name="rmsnorm_in",
    )(xp, xs, wn)


def _proj_uv_kernel(h_ref, wa_ref, wb_ref, uv_ref, wa_bf, wb_bf):
    j = pl.program_id(0)
    i = pl.program_id(1)

    @pl.when(i == 0)
    def _():
        wa_bf[...] = wa_ref[...].astype(BF16)

    @pl.when((i == 0) & (j == 1))
    def _():
        wb_bf[...] = wb_ref[...].astype(BF16)

    def proj(w_bf):
        return jnp.dot(h_ref[...], w_bf[...], preferred_element_type=F32)

    @pl.when(j == 0)
    def _():
        uv_ref[...] = proj(wa_bf)

    @pl.when(j == 1)
    def _():
        uv_ref[...] = proj(wa_bf) * _sigmoid(proj(wb_bf))


def _proj_uv_call(h, w_in, layer, *, tm, pw):
    n, d = h.shape
    return pl.pallas_call(
        _proj_uv_kernel,
        grid=(2, n // tm),
        in_specs=[pl.BlockSpec((tm, d), lambda j, i: (i, 0)),
                  pl.BlockSpec((None, d, pw), lambda j, i: (layer, 0, 2 * j)),
                  pl.BlockSpec((None, d, pw), lambda j, i: (layer, 0, 3),
                               pipeline_mode=pl.Buffered(1))],
        out_specs=pl.BlockSpec((tm, pw), lambda j, i: (i, j)),
        out_shape=jax.ShapeDtypeStruct((n, 2 * pw), F32),
        scratch_shapes=[pltpu.VMEM((d, pw), BF16), pltpu.VMEM((d, pw), BF16)],
        compiler_params=pltpu.CompilerParams(dimension_semantics=("arbitrary", "arbitrary"),
                                             vmem_limit_bytes=VMEM_LIMIT_BYTES),
        name="proj_uv",
    )(h, w_in, w_in)


def _conv_block(ext_ref, c, start, wv, bv, prev, never):
    acc = bv if prev is None else jnp.where(never, prev, bv)
    for k, w in enumerate(wv):
        acc = acc + w * ext_ref[c, start + k:start + k + CONV_ROWS, :]
    return acc


def _proj_gate_kernel(h_ref, w_ref, v_ref, cconv_ref, cw_ref, cb_ref, g_ref, cv_ref, w_bf, ext,
                      *, n_p, seg, hpv, conv_k):
    j = pl.program_id(0)
    i = pl.program_id(1)
    tm = h_ref.shape[0]
    n_c = ext.shape[0]
    does_conv = j >= 2
    never = i < 0

    @pl.when(i == 0)
    def _():
        w_bf[...] = w_ref[...].astype(BF16)

    @pl.when(does_conv & (i == 0))
    def _():
        ext[:, 0:hpv, :] = jnp.zeros((n_c, hpv, LANES), F32)

    @pl.when(does_conv & (i > 0) & (i < n_p))
    def _():
        ext[:, 0:hpv, :] = ext[:, tm:tm + hpv, :]

    def proj():
        return jnp.dot(h_ref[...], w_bf[...], preferred_element_type=F32)

    def taps(c):
        cols = slice(LANES * c, LANES * (c + 1))
        wv = [jnp.broadcast_to(cw_ref[k:k + 1, cols], (CONV_ROWS, LANES)) for k in range(conv_k)]
        return wv, jnp.broadcast_to(cb_ref[0:1, cols], (CONV_ROWS, LANES))

    @pl.when(j < 2)
    def _():
        z = proj()
        g_ref[...] = (z * _sigmoid(z)).astype(BF16)

    @pl.when(does_conv & (i < n_p))
    def _():
        g_ref[...] = _sigmoid(proj()).astype(BF16)
        for c in range(n_c):
            cols = slice(LANES * c, LANES * (c + 1))
            ext[c, hpv:hpv + tm, :] = v_ref[:, cols]
            wv, bv = taps(c)
            acc = None
            for r in range(0, tm, CONV_ROWS):
                acc = _conv_block(ext, c, hpv - (conv_k - 1) + r, wv, bv, acc, never)
                cv_ref[r:r + CONV_ROWS, cols] = acc

    @pl.when(does_conv & (i >= n_p))
    def _():
        g_ref[...] = _sigmoid(proj()).astype(BF16)
        spt = tm // seg
        for c in range(n_c):
            cols = slice(LANES * c, LANES * (c + 1))
            wv, bv = taps(c)
            acc = None
            for s in range(spt):
                q = (i - n_p) * spt + s
                base = s * (hpv + seg)
                ext[c, base:base + hpv, :] = cconv_ref[q, :, cols]
                ext[c, base + hpv:base + hpv + seg, :] = v_ref[seg * s:seg * (s + 1), cols]
                for r in range(0, seg, CONV_ROWS):
                    acc = _conv_block(ext, c, base + hpv - (conv_k - 1) + r, wv, bv, acc, never)
                    cv_ref[seg * s + r:seg * s + r + CONV_ROWS, cols] = acc


def _proj_gate_call(h, w_in, uv, cconv, cw, cb, layer, *, tm, pw, n_prompt_rows, seg):
    n, d = h.shape
    n_i = n // tm
    n_p = n_prompt_rows // tm
    hpv = cconv.shape[2]
    conv_k = cw.shape[1]
    cc = CONV_CHUNKS_PER_JOB * LANES
    n_conv_jobs = pw // cc
    n_j = (2 * pw + 2 * d) // pw
    assert n_conv_jobs == n_j - 2 and n_prompt_rows % tm == 0 and tm % seg == 0 and seg % CONV_ROWS == 0
    assert tm % CONV_ROWS == 0 and seg >= hpv
    spt = tm // seg

    cjob = lambda j: jnp.maximum(j - 2, 0)
    w_map = lambda j, i: (layer, 0, jnp.where(j == 0, 1, jnp.where(j == 1, 4, j + 3)))
    return pl.pallas_call(
        functools.partial(_proj_gate_kernel, n_p=n_p, seg=seg, hpv=hpv, conv_k=conv_k),
        grid=(n_j, n_i),
        in_specs=[pl.BlockSpec((tm, d), lambda j, i: (i, 0)),
                  pl.BlockSpec((None, d, pw), w_map),
                  pl.BlockSpec((tm, cc), lambda j, i: (jnp.where(j >= 2, i, 0), pw // cc + cjob(j))),
                  pl.BlockSpec((None,) + cconv.shape[1:3] + (cc,), lambda j, i: (layer, 0, 0, cjob(j))),
                  pl.BlockSpec((None, conv_k, cc), lambda j, i: (layer, 0, cjob(j))),
                  pl.BlockSpec((None, 1, cc), lambda j, i: (layer, 0, cjob(j)))],
        out_specs=[pl.BlockSpec((tm, pw), lambda j, i: (i, j)),
                   pl.BlockSpec((tm, cc), lambda j, i: (jnp.where(j >= 2, i, 0), cjob(j)))],
        out_shape=[jax.ShapeDtypeStruct((n, 2 * pw + 2 * d), BF16),
                   jax.ShapeDtypeStruct((n, pw), F32)],
        scratch_shapes=[pltpu.VMEM((d, pw), BF16),
                        pltpu.VMEM((CONV_CHUNKS_PER_JOB, max(hpv + tm, spt * (hpv + seg)), LANES), F32)],
        compiler_params=pltpu.CompilerParams(dimension_semantics=("arbitrary", "arbitrary"),
                                             vmem_limit_bytes=VMEM_LIMIT_BYTES),
        name="proj_gate",
    )(h, w_in, uv, cconv, cw, cb)


def _mixer_kernel(*refs, tm, n_p, seg, split_x, last, hpu):
    n_x = 2 if split_x else 1
    u_ref, cv_ref, sgp_ref, sgc_ref, smp_ref, smc_ref = refs[:6]
    x_refs = refs[6:6 + n_x]
    (cpool_ref, wmix_ref, wpo_ref, wco_ref, wout_ref, pscale_ref, lng_ref, lnb_ref,
     wn_ref) = refs[6 + n_x:15 + n_x]
    out_refs = refs[15 + n_x:17 + n_x]
    extu, pooled_scr = refs[17 + n_x:]
    n_chunks = u_ref.shape[1] // LANES
    grp_chunks = n_chunks // len(POOL_WINDOWS)
    i = pl.program_id(0)

    def pool_rows(base, nrows, pos0, out_row0):
        for g, w in enumerate(POOL_WINDOWS):
            for cc in range(grp_chunks):
                c = g * grp_chunks + cc
                cur = extu[c, base + hpu:base + hpu + nrows, :]
                s = cur
                for o in range(1, w):
                    s = s + extu[c, base + hpu - o:base + hpu - o + nrows, :]
                if pos0 is None:
                    mean = s * (1.0 / w)
                else:
                    pos = pos0 + lax.broadcasted_iota(jnp.int32, (nrows, LANES), 0)
                    mean = s / jnp.minimum(w, pos + 1).astype(F32)
                pooled_scr[out_row0:out_row0 + nrows, LANES * c:LANES * (c + 1)] = mean - cur

    def tail(x_ref, o_refs):
        pooled = pooled_scr[...]
        gw = pooled.shape[1] // len(POOL_WINDOWS)
        mixed = jnp.concatenate(
            [jnp.dot(pooled[:, gw * g:gw * (g + 1)].astype(BF16), wmix_ref[g], preferred_element_type=F32)
             for g in range(len(POOL_WINDOWS))], axis=1)
        pool_in = (mixed * pscale_ref[...] * sgp_ref[...].astype(F32)).astype(BF16)
        pool_branch = jnp.dot(pool_in, wpo_ref[...], preferred_element_type=F32)

        cv = cv_ref[...]
        mu = jnp.mean(cv, axis=-1, keepdims=True)
        xc = cv - mu
        var = jnp.mean(xc * xc, axis=-1, keepdims=True)
        ln = xc * lax.rsqrt(var + LN_EPS) * lng_ref[...] + lnb_ref[...]
        conv_in = (ln * _sigmoid(ln) * sgc_ref[...].astype(F32)).astype(BF16)
        conv_branch = jnp.dot(conv_in, wco_ref[...], preferred_element_type=F32)

        merged = (smp_ref[...].astype(F32) * pool_branch + smc_ref[...].astype(F32) * conv_branch).astype(BF16)
        y = x_ref[...] + jnp.dot(merged, wout_ref[...], preferred_element_type=F32)
        ms = jnp.mean(y * y, axis=-1, keepdims=True)
        hn = y * lax.rsqrt(ms + RMS_EPS) * wn_ref[...]
        if last:
            o_refs[0][...] = hn
        else:
            o_refs[0][...] = y
            o_refs[1][...] = hn.astype(BF16)

    @pl.when(i < n_p)
    def _():
        @pl.when(i == 0)
        def _():
            extu[:, 0:hpu, :] = jnp.zeros((n_chunks, hpu, LANES), F32)

        @pl.when(i > 0)
        def _():
            extu[:, 0:hpu, :] = extu[:, tm:tm + hpu, :]

        for c in range(n_chunks):
            extu[c, hpu:hpu + tm, :] = u_ref[:, LANES * c:LANES * (c + 1)]
        pool_rows(0, tm, i * tm, 0)
        tail(x_refs[0], out_refs if not last else out_refs[0:1])

    @pl.when(i >= n_p)
    def _():
        spt = tm // seg
        for s in range(spt):
            q = (i - n_p) * spt + s
            bu = s * (hpu + seg)
            for c in range(n_chunks):
                cols = slice(LANES * c, LANES * (c + 1))
                extu[c, bu:bu + hpu, :] = cpool_ref[q, :, cols]
                extu[c, bu + hpu:bu + hpu + seg, :] = u_ref[seg * s:seg * (s + 1), cols]
            pool_rows(bu, seg, None, seg * s)
        tail(x_refs[-1], out_refs if not last else out_refs[1:2])


def _mixer_call(uv, cv, gates, xs, layer, cpool, wmix, wpo, wco, wout, pscale, lng, lnb, wn,
                *, tm, n_prompt_rows, seg, last):
    n = uv.shape[0]
    pw = uv.shape[1] // 2
    d = wout.shape[2]
    n_p = n_prompt_rows // tm
    n_s = (n - n_prompt_rows) // tm
    split_x = len(xs) == 2
    hpu = cpool.shape[2]
    spt = tm // seg
    assert PAST_LEN >= max(POOL_WINDOWS) - 1 and seg >= hpu and n_prompt_rows % tm == 0 and tm % seg == 0

    def const(a):
        rest = a.shape[1:]
        return pl.BlockSpec((None,) + rest, lambda i: (layer,) + (0,) * len(rest),
                            pipeline_mode=pl.Buffered(1))

    row = lambda c: (lambda i: (i, c))
    if split_x:
        x_specs = [pl.BlockSpec((tm, d), lambda i: (jnp.minimum(i, n_p - 1), 0)),
                   pl.BlockSpec((tm, d), lambda i: (jnp.maximum(i - n_p, 0), 0))]
    else:
        x_specs = [pl.BlockSpec((tm, d), row(0))]
    in_specs = ([pl.BlockSpec((tm, pw), row(0)), pl.BlockSpec((tm, pw), row(0)),
                 pl.BlockSpec((tm, pw), row(0)), pl.BlockSpec((tm, pw), row(1)),
                 pl.BlockSpec((tm, d), row(1)), pl.BlockSpec((tm, d), row(2))]
                + x_specs
                + [const(a) for a in (cpool, wmix, wpo, wco, wout, pscale, lng, lnb, wn)])
    if last:
        out_specs = [pl.BlockSpec((tm, d), lambda i: (jnp.minimum(i, n_p - 1), 0)),
                     pl.BlockSpec((tm, d), lambda i: (jnp.maximum(i - n_p, 0), 0))]
        out_shape = [jax.ShapeDtypeStruct((n_prompt_rows, d), F32),
                     jax.ShapeDtypeStruct((n - n_prompt_rows, d), F32)]
    else:
        out_specs = [pl.BlockSpec((tm, d), row(0)), pl.BlockSpec((tm, d), row(0))]
        out_shape = [jax.ShapeDtypeStruct((n, d), F32), jax.ShapeDtypeStruct((n, d), BF16)]
    n_chunks = pw // LANES
    rows_u = max(hpu + tm, spt * (hpu + seg))
    kern = functools.partial(_mixer_kernel, tm=tm, n_p=n_p, seg=seg, split_x=split_x, last=last, hpu=hpu)
    return pl.pallas_call(
        kern,
        grid=(n_p + n_s,),
        in_specs=in_specs,
        out_specs=out_specs,
        out_shape=out_shape,
        scratch_shapes=[pltpu.VMEM((n_chunks, rows_u, LANES), F32),
                        pltpu.VMEM((tm, pw), F32)],
        compiler_params=pltpu.CompilerParams(dimension_semantics=("arbitrary",),
                                             vmem_limit_bytes=VMEM_LIMIT_BYTES),
        name="mixer_last" if last else "mixer",
    )(uv, cv, gates, gates, gates, gates, *xs, cpool, wmix, wpo, wco, wout, pscale, lng, lnb, wn)


def kernel(x_prompt, x_sample, cache_pool, cache_conv, w_norm, w_in, w_pool_mix, pool_scale, w_pool_out,
           conv_w, conv_b, ln_g, ln_b, w_conv_out, w_out, w_final_norm):
    b, t, d = x_prompt.shape
    sb, st, _ = x_sample.shape
    depth = w_norm.shape[0]
    pw = pool_scale.shape[1]
    hu, hv = cache_pool.shape[2], cache_conv.shape[2]
    assert b == 1 and conv_w.shape[2] == pw and 2 * pw == d and PROJ_TN == pw
    assert max(POOL_WINDOWS) - 1 == hu and conv_w.shape[1] - 1 == hv
    n_prompt = b * t
    xp = x_prompt.reshape(n_prompt, d)
    xs = x_sample.reshape(sb * st, d)
    hpu, hpv = _round_up(hu, SUBLANES), _round_up(hv, SUBLANES)
    cpool = jnp.pad(cache_pool, ((0, 0), (0, 0), (hpu - hu, 0), (0, 0)))
    cconv = jnp.pad(cache_conv, ((0, 0), (0, 0), (hpv - hv, 0), (0, 0)))
    row2 = lambda a: a.reshape(depth, 1, a.shape[-1])
    pscale, cb, lng, lnb = row2(pool_scale), row2(conv_b), row2(ln_g), row2(ln_b)
    wnext = jnp.concatenate([w_norm[1:], w_final_norm[None]], axis=0).reshape(depth, 1, d)
    wmix, wpo, wco, wout = (w.astype(BF16) for w in (w_pool_mix, w_pool_out, w_conv_out, w_out))

    h = _norm_call(xp, xs, w_norm[0:1], tm=NORM_TM)
    x_parts = (xp, xs)
    pool_p, conv_p, pool_s, conv_s = [], [], [], []
    outs = None
    for l in range(depth):
        uv = _proj_uv_call(h, w_in, l, tm=PROJ_UV_TM, pw=pw)
        gates, cv = _proj_gate_call(h, w_in, uv, cconv, conv_w, cb, l, tm=PROJ_GATE_TM, pw=pw,
                                    n_prompt_rows=n_prompt, seg=st)
        last = l == depth - 1
        outs = _mixer_call(uv, cv, gates, x_parts, l, cpool, wmix, wpo, wco, wout, pscale, lng, lnb, wnext,
                           tm=MIXER_TM, n_prompt_rows=n_prompt, seg=st, last=last)
        if not last:
            x_parts = (outs[0],)
            h = outs[1]
        uvs = uv[n_prompt:].reshape(sb, st, 2 * pw)
        pool_p.append(uv[n_prompt - hu:n_prompt, :pw].reshape(b, hu, pw))
        conv_p.append(uv[n_prompt - hv:n_prompt, pw:].reshape(b, hv, pw))
        pool_s.append(uvs[:, st - hu:, :pw])
        conv_s.append(uvs[:, st - hv:, pw:])
    y_prompt = outs[0].reshape(b, t, d)
    y_sample = outs[1].reshape(sb, st, d)
    return (y_prompt, y_sample, jnp.stack(pool_p), jnp.stack(conv_p), jnp.stack(pool_s), jnp.stack(conv_s))
```

```python
import functools

import jax
import jax.numpy as jnp
from jax import lax
from jax.experimental import pallas as pl
from jax.experimental.pallas import tpu as pltpu

PAST_LEN = 2048
POOL_WINDOWS = (2, 4, 8, 16)
RMS_EPS = 1e-6
LN_EPS = 1e-5

LANES = 128
SUBLANES = 8
VMEM_LIMIT_BYTES = 58 * 1024 * 1024

NORM_TM = 512
PROJ_UV_TM = 1088
PROJ_GATE_TM = 1088
PROJ_TN = 1024
CONV_CHUNKS_PER_JOB = 2
CONV_ROWS = SUBLANES
MIXER_TM = 256

BF16 = jnp.bfloat16
F32 = jnp.float32


def _round_up(n, m):
    return (n + m - 1) // m * m


def _sigmoid(x):
    return 0.5 * jnp.tanh(0.5 * x) + 0.5


def _sigmoid_exp(x):
    return jax.nn.sigmoid(x)


def _norm_kernel(xp_ref, xs_ref, wn_ref, h_ref, *, n_p):
    i = pl.program_id(0)

    def emit(x_ref):
        x = x_ref[...]
        ms = jnp.mean(x * x, axis=-1, keepdims=True)
        h_ref[...] = (x * lax.rsqrt(ms + RMS_EPS) * wn_ref[...]).astype(BF16)

    @pl.when(i < n_p)
    def _():
        emit(xp_ref)

    @pl.when(i >= n_p)
    def _():
        emit(xs_ref)


def _norm_call(xp, xs, wn, *, tm):
    n_p, n_s = xp.shape[0] // tm, xs.shape[0] // tm
    d = xp.shape[1]
    return pl.pallas_call(
        functools.partial(_norm_kernel, n_p=n_p),
        grid=(n_p + n_s,),
        in_specs=[pl.BlockSpec((tm, d), lambda i: (jnp.minimum(i, n_p - 1), 0)),
                  pl.BlockSpec((tm, d), lambda i: (jnp.maximum(i - n_p, 0), 0)),
                  pl.BlockSpec((1, d), lambda i: (0, 0))],
        out_specs=pl.BlockSpec((tm, d), lambda i: (i, 0)),
        out_shape=jax.ShapeDtypeStruct((xp.shape[0] + xs.shape[0], d), BF16),
        compiler_params=pltpu.CompilerParams(dimension_semantics=("arbitrary",)),
        name="rmsnorm_in",
    )(xp, xs, wn)


def _proj_uv_kernel(h_ref, wa_ref, wb_ref, uv_ref, wa_bf, wb_bf):
    j = pl.program_id(0)
    i = pl.program_id(1)

    @pl.when(i == 0)
    def _():
        wa_bf[...] = wa_ref[...].astype(BF16)

    @pl.when((i == 0) & (j == 1))
    def _():
        wb_bf[...] = wb_ref[...].astype(BF16)

    def proj(w_bf):
        return jnp.dot(h_ref[...], w_bf[...], preferred_element_type=F32)

    @pl.when(j == 0)
    def _():
        uv_ref[...] = proj(wa_bf)

    @pl.when(j == 1)
    def _():
        uv_ref[...] = proj(wa_bf) * _sigmoid(proj(wb_bf))


def _proj_uv_call(h, w_in, layer, *, tm, pw):
    n, d = h.shape
    return pl.pallas_call(
        _proj_uv_kernel,
        grid=(2, n // tm),
        in_specs=[pl.BlockSpec((tm, d), lambda j, i: (i, 0)),
                  pl.BlockSpec((None, d, pw), lambda j, i: (layer, 0, 2 * j),
                               pipeline_mode=pl.Buffered(1)),
                  pl.BlockSpec((None, d, pw), lambda j, i: (layer, 0, 3),
                               pipeline_mode=pl.Buffered(1))],
        out_specs=pl.BlockSpec((tm, pw), lambda j, i: (i, j)),
        out_shape=jax.ShapeDtypeStruct((n, 2 * pw), F32),
        scratch_shapes=[pltpu.VMEM((d, pw), BF16), pltpu.VMEM((d, pw), BF16)],
        compiler_params=pltpu.CompilerParams(dimension_semantics=("arbitrary", "arbitrary"),
                                             vmem_limit_bytes=VMEM_LIMIT_BYTES),
        name="proj_uv",
    )(h, w_in, w_in)


def _conv_block(ext_ref, c, start, wv, bv, prev, never):
    acc = bv if prev is None else jnp.where(never, prev, bv)
    for k, w in enumerate(wv):
        acc = acc + w * ext_ref[c, start + k:start + k + CONV_ROWS, :]
    return acc


def _proj_gate_kernel(h_ref, w_ref, v_ref, cconv_ref, cw_ref, cb_ref, g_ref, cv_ref, w_bf, ext,
                      *, n_full, rem, seg, hpv, conv_k):
    j = pl.program_id(0)
    i = pl.program_id(1)
    tm = h_ref.shape[0]
    n_c = ext.shape[0]
    does_conv = j >= 2
    never = i < 0

    @pl.when(i == 0)
    def _():
        w_bf[...] = w_ref[...].astype(BF16)

    @pl.when(does_conv & (i == 0))
    def _():
        ext[:, 0:hpv, :] = jnp.zeros((n_c, hpv, LANES), F32)

    @pl.when(does_conv & (i > 0))
    def _():
        ext[:, 0:hpv, :] = ext[:, tm:tm + hpv, :]

    def proj():
        return jnp.dot(h_ref[...], w_bf[...], preferred_element_type=F32)

    def conv_rows(c, acc, wv, bv, ext_row0, out_row0, nrows):
        cols = slice(LANES * c, LANES * (c + 1))
        for r in range(0, nrows, CONV_ROWS):
            acc = _conv_block(ext, c, ext_row0 - (conv_k - 1) + r, wv, bv, acc, never)
            cv_ref[out_row0 + r:out_row0 + r + CONV_ROWS, cols] = acc
        return acc

    def conv_tile(n_prompt_rows):
        n_seg = (tm - n_prompt_rows) // seg
        for c in range(n_c):
            cols = slice(LANES * c, LANES * (c + 1))
            wv = [jnp.broadcast_to(cw_ref[k:k + 1, cols], (CONV_ROWS, LANES)) for k in range(conv_k)]
            bv = jnp.broadcast_to(cb_ref[0:1, cols], (CONV_ROWS, LANES))
            acc = None
            if n_prompt_rows:
                ext[c, hpv:hpv + n_prompt_rows, :] = v_ref[0:n_prompt_rows, cols]
                acc = conv_rows(c, acc, wv, bv, hpv, 0, n_prompt_rows)
            for s in range(n_seg):
                base = hpv + n_prompt_rows + s * (hpv + seg)
                row0 = n_prompt_rows + seg * s
                ext[c, base:base + hpv, :] = cconv_ref[s, :, cols]
                ext[c, base + hpv:base + hpv + seg, :] = v_ref[row0:row0 + seg, cols]
                acc = conv_rows(c, acc, wv, bv, base + hpv, row0, seg)

    @pl.when(j < 2)
    def _():
        z = proj()
        g_ref[...] = (z * _sigmoid(z)).astype(BF16)

    @pl.when(does_conv & (i < n_full))
    def _():
        g_ref[...] = _sigmoid_exp(proj()).astype(BF16)
        conv_tile(tm)

    @pl.when(does_conv & (i >= n_full))
    def _():
        g_ref[...] = _sigmoid_exp(proj()).astype(BF16)
        conv_tile(rem)


def _proj_gate_call(h, w_in, uv, cconv, cw, cb, layer, *, tm, pw, n_prompt_rows, seg):
    n, d = h.shape
    n_i = n // tm
    n_full, rem = divmod(n_prompt_rows, tm)
    n_seg = (n - n_prompt_rows) // seg
    hpv = cconv.shape[2]
    conv_k = cw.shape[1]
    cc = CONV_CHUNKS_PER_JOB * LANES
    n_conv_jobs = pw // cc
    n_j = (2 * pw + 2 * d) // pw
    assert n_conv_jobs == n_j - 2 and n_full + 1 == n_i and rem + n_seg * seg == tm and n_seg == cconv.shape[1]
    assert rem % CONV_ROWS == 0 and seg % CONV_ROWS == 0 and tm % CONV_ROWS == 0 and seg >= hpv

    cjob = lambda j: jnp.maximum(j - 2, 0)
    w_map = lambda j, i: (layer, 0, jnp.where(j == 0, 1, jnp.where(j == 1, 4, j + 3)))
    return pl.pallas_call(
        functools.partial(_proj_gate_kernel, n_full=n_full, rem=rem, seg=seg, hpv=hpv, conv_k=conv_k),
        grid=(n_j, n_i),
        in_specs=[pl.BlockSpec((tm, d), lambda j, i: (i, 0)),
                  pl.BlockSpec((None, d, pw), w_map),
                  pl.BlockSpec((tm, cc), lambda j, i: (jnp.where(j >= 2, i, 0), pw // cc + cjob(j))),
                  pl.BlockSpec((None,) + cconv.shape[1:3] + (cc,), lambda j, i: (layer, 0, 0, cjob(j))),
                  pl.BlockSpec((None, conv_k, cc), lambda j, i: (layer, 0, cjob(j))),
                  pl.BlockSpec((None, 1, cc), lambda j, i: (layer, 0, cjob(j)))],
        out_specs=[pl.BlockSpec((tm, pw), lambda j, i: (i, j)),
                   pl.BlockSpec((tm, cc), lambda j, i: (jnp.where(j >= 2, i, 0), cjob(j)))],
        out_shape=[jax.ShapeDtypeStruct((n, 2 * pw + 2 * d), BF16),
                   jax.ShapeDtypeStruct((n, pw), F32)],
        scratch_shapes=[pltpu.VMEM((d, pw), BF16),
                        pltpu.VMEM((CONV_CHUNKS_PER_JOB, hpv + tm + n_seg * hpv, LANES), F32)],
        compiler_params=pltpu.CompilerParams(dimension_semantics=("arbitrary", "arbitrary"),
                                             vmem_limit_bytes=VMEM_LIMIT_BYTES),
        name="proj_gate",
    )(h, w_in, uv, cconv, cw, cb)


def _mixer_kernel(*refs, tm, n_p, seg, split_x, last, hpu):
    n_x = 2 if split_x else 1
    u_ref, cv_ref, sgp_ref, sgc_ref, smp_ref, smc_ref = refs[:6]
    x_refs = refs[6:6 + n_x]
    (cpool_ref, wmix_ref, wpo_ref, wco_ref, wout_ref, pscale_ref, lng_ref, lnb_ref,
     wn_ref) = refs[6 + n_x:15 + n_x]
    out_refs = refs[15 + n_x:17 + n_x]
    extu, pooled_scr = refs[17 + n_x:]
    n_chunks = u_ref.shape[1] // LANES
    grp_chunks = n_chunks // len(POOL_WINDOWS)
    i = pl.program_id(0)

    def pool_rows(base, nrows, pos0, out_row0):
        for g, w in enumerate(POOL_WINDOWS):
            for cc in range(grp_chunks):
                c = g * grp_chunks + cc
                cur = extu[c, base + hpu:base + hpu + nrows, :]
                s = cur
                for o in range(1, w):
                    s = s + extu[c, base + hpu - o:base + hpu - o + nrows, :]
                if pos0 is None:
                    mean = s * (1.0 / w)
                else:
                    pos = pos0 + lax.broadcasted_iota(jnp.int32, (nrows, LANES), 0)
                    mean = s / jnp.minimum(w, pos + 1).astype(F32)
                pooled_scr[out_row0:out_row0 + nrows, LANES * c:LANES * (c + 1)] = mean - cur

    def tail(x_ref, o_refs):
        pooled = pooled_scr[...]
        gw = pooled.shape[1] // len(POOL_WINDOWS)
        mixed = jnp.concatenate(
            [jnp.dot(pooled[:, gw * g:gw * (g + 1)].astype(BF16), wmix_ref[g], preferred_element_type=F32)
             for g in range(len(POOL_WINDOWS))], axis=1)
        pool_in = (mixed * pscale_ref[...] * sgp_ref[...].astype(F32)).astype(BF16)
        pool_branch = jnp.dot(pool_in, wpo_ref[...], preferred_element_type=F32)

        cv = cv_ref[...]
        mu = jnp.mean(cv, axis=-1, keepdims=True)
        xc = cv - mu
        var = jnp.mean(xc * xc, axis=-1, keepdims=True)
        ln = xc * lax.rsqrt(var + LN_EPS) * lng_ref[...] + lnb_ref[...]
        conv_in = (ln * _sigmoid(ln) * sgc_ref[...].astype(F32)).astype(BF16)
        conv_branch = jnp.dot(conv_in, wco_ref[...], preferred_element_type=F32)

        merged = (smp_ref[...].astype(F32) * pool_branch + smc_ref[...].astype(F32) * conv_branch).astype(BF16)
        y = x_ref[...] + jnp.dot(merged, wout_ref[...], preferred_element_type=F32)
        ms = jnp.mean(y * y, axis=-1, keepdims=True)
        hn = y * lax.rsqrt(ms + RMS_EPS) * wn_ref[...]
        if last:
            o_refs[0][...] = hn
        else:
            o_refs[0][...] = y
            o_refs[1][...] = hn.astype(BF16)

    @pl.when(i < n_p)
    def _():
        @pl.when(i == 0)
        def _():
            extu[:, 0:hpu, :] = jnp.zeros((n_chunks, hpu, LANES), F32)

        @pl.when(i > 0)
        def _():
            extu[:, 0:hpu, :] = extu[:, tm:tm + hpu, :]

        for c in range(n_chunks):
            extu[c, hpu:hpu + tm, :] = u_ref[:, LANES * c:LANES * (c + 1)]
        pool_rows(0, tm, i * tm, 0)
        tail(x_refs[0], out_refs if not last else out_refs[0:1])

    @pl.when(i >= n_p)
    def _():
        spt = tm // seg
        for s in range(spt):
            q = (i - n_p) * spt + s
            bu = s * (hpu + seg)
            for c in range(n_chunks):
                cols = slice(LANES * c, LANES * (c + 1))
                extu[c, bu:bu + hpu, :] = cpool_ref[q, :, cols]
                extu[c, bu + hpu:bu + hpu + seg, :] = u_ref[seg * s:seg * (s + 1), cols]
            pool_rows(bu, seg, None, seg * s)
        tail(x_refs[-1], out_refs if not last else out_refs[1:2])


def _mixer_call(uv, cv, gates, xs, layer, cpool, wmix, wpo, wco, wout, pscale, lng, lnb, wn,
                *, tm, n_prompt_rows, seg, last):
    n = uv.shape[0]
    pw = uv.shape[1] // 2
    d = wout.shape[2]
    n_p = n_prompt_rows // tm
    n_s = (n - n_prompt_rows) // tm
    split_x = len(xs) == 2
    hpu = cpool.shape[2]
    spt = tm // seg
    assert PAST_LEN >= max(POOL_WINDOWS) - 1 and seg >= hpu and n_prompt_rows % tm == 0 and tm % seg == 0

    def const(a):
        rest = a.shape[1:]
        return pl.BlockSpec((None,) + rest, lambda i: (layer,) + (0,) * len(rest),
                            pipeline_mode=pl.Buffered(1))

    row = lambda c: (lambda i: (i, c))
    if split_x:
        x_specs = [pl.BlockSpec((tm, d), lambda i: (jnp.minimum(i, n_p - 1), 0)),
                   pl.BlockSpec((tm, d), lambda i: (jnp.maximum(i - n_p, 0), 0))]
    else:
        x_specs = [pl.BlockSpec((tm, d), row(0))]
    in_specs = ([pl.BlockSpec((tm, pw), row(0)), pl.BlockSpec((tm, pw), row(0)),
                 pl.BlockSpec((tm, pw), row(0)), pl.BlockSpec((tm, pw), row(1)),
                 pl.BlockSpec((tm, d), row(1)), pl.BlockSpec((tm, d), row(2))]
                + x_specs
                + [const(a) for a in (cpool, wmix, wpo, wco, wout, pscale, lng, lnb, wn)])
    if last:
        out_specs = [pl.BlockSpec((tm, d), lambda i: (jnp.minimum(i, n_p - 1), 0)),
                     pl.BlockSpec((tm, d), lambda i: (jnp.maximum(i - n_p, 0), 0))]
        out_shape = [jax.ShapeDtypeStruct((n_prompt_rows, d), F32),
                     jax.ShapeDtypeStruct((n - n_prompt_rows, d), F32)]
    else:
        out_specs = [pl.BlockSpec((tm, d), row(0)), pl.BlockSpec((tm, d), row(0))]
        out_shape = [jax.ShapeDtypeStruct((n, d), F32), jax.ShapeDtypeStruct((n, d), BF16)]
    n_chunks = pw // LANES
    rows_u = max(hpu + tm, spt * (hpu + seg))
    kern = functools.partial(_mixer_kernel, tm=tm, n_p=n_p, seg=seg, split_x=split_x, last=last, hpu=hpu)
    return pl.pallas_call(
        kern,
        grid=(n_p + n_s,),
        in_specs=in_specs,
        out_specs=out_specs,
        out_shape=out_shape,
        scratch_shapes=[pltpu.VMEM((n_chunks, rows_u, LANES), F32),
                        pltpu.VMEM((tm, pw), F32)],
        compiler_params=pltpu.CompilerParams(dimension_semantics=("arbitrary",),
                                             vmem_limit_bytes=VMEM_LIMIT_BYTES),
        name="mixer_last" if last else "mixer",
    )(uv, cv, gates, gates, gates, gates, *xs, cpool, wmix, wpo, wco, wout, pscale, lng, lnb, wn)


def kernel(x_prompt, x_sample, cache_pool, cache_conv, w_norm, w_in, w_pool_mix, pool_scale, w_pool_out,
           conv_w, conv_b, ln_g, ln_b, w_conv_out, w_out, w_final_norm):
    b, t, d = x_prompt.shape
    sb, st, _ = x_sample.shape
    depth = w_norm.shape[0]
    pw = pool_scale.shape[1]
    hu, hv = cache_pool.shape[2], cache_conv.shape[2]
    assert b == 1 and conv_w.shape[2] == pw and 2 * pw == d and PROJ_TN == pw
    assert max(POOL_WINDOWS) - 1 == hu and conv_w.shape[1] - 1 == hv
    n_prompt = b * t
    xp = x_prompt.reshape(n_prompt, d)
    xs = x_sample.reshape(sb * st, d)
    hpu, hpv = _round_up(hu, SUBLANES), _round_up(hv, SUBLANES)
    cpool = jnp.pad(cache_pool, ((0, 0), (0, 0), (hpu - hu, 0), (0, 0)))
    cconv = jnp.pad(cache_conv, ((0, 0), (0, 0), (hpv - hv, 0), (0, 0)))
    row2 = lambda a: a.reshape(depth, 1, a.shape[-1])
    pscale, cb, lng, lnb = row2(pool_scale), row2(conv_b), row2(ln_g), row2(ln_b)
    wnext = jnp.concatenate([w_norm[1:], w_final_norm[None]], axis=0).reshape(depth, 1, d)
    wmix, wpo, wco, wout = (w.astype(BF16) for w in (w_pool_mix, w_pool_out, w_conv_out, w_out))

    h = _norm_call(xp, xs, w_norm[0:1], tm=NORM_TM)
    x_parts = (xp, xs)
    pool_p, conv_p, pool_s, conv_s = [], [], [], []
    outs = None
    for l in range(depth):
        uv = _proj_uv_call(h, w_in, l, tm=PROJ_UV_TM, pw=pw)
        gates, cv = _proj_gate_call(h, w_in, uv, cconv, conv_w, cb, l, tm=PROJ_GATE_TM, pw=pw,
                                    n_prompt_rows=n_prompt, seg=st)
        last = l == depth - 1
        outs = _mixer_call(uv, cv, gates, x_parts, l, cpool, wmix, wpo, wco, wout, pscale, lng, lnb, wnext,
                           tm=MIXER_TM, n_prompt_rows=n_prompt, seg=st, last=last)
        if not last:
            x_parts = (outs[0],)
            h = outs[1]
        uvs = uv[n_prompt:].reshape(sb, st, 2 * pw)
        pool_p.append(uv[n_prompt - hu:n_prompt, :pw].reshape(b, hu, pw))
        conv_p.append(uv[n_prompt - hv:n_prompt, pw:].reshape(b, hv, pw))
        pool_s.append(uvs[:, st - hu:, :pw])
        conv_s.append(uvs[:, st - hv:, pw:])
    y_prompt = outs[0].reshape(b, t, d)
    y_sample = outs[1].reshape(sb, st, d)
    return (y_prompt, y_sample, jnp.stack(pool_p), jnp.stack(conv_p), jnp.stack(pool_s), jnp.stack(conv_s))
```

```python
import functools

import jax
import jax.numpy as jnp
from jax import lax
from jax.experimental import pallas as pl
from jax.experimental.pallas import tpu as pltpu

PAST_LEN = 2048
POOL_WINDOWS = (2, 4, 8, 16)
RMS_EPS = 1e-6
LN_EPS = 1e-5

LANES = 128
SUBLANES = 8
VMEM_LIMIT_BYTES = 58 * 1024 * 1024

NORM_TM = 512
PROJ_UV_TM = 1088
PROJ_GATE_TM = 1088
PROJ_TN = 1024
CONV_CHUNKS_PER_JOB = 2
CONV_ROWS = SUBLANES
MIXER_TM = 256

BF16 = jnp.bfloat16
F32 = jnp.float32


def _round_up(n, m):
    return (n + m - 1) // m * m


def _sigmoid(x):
    return 0.5 * jnp.tanh(0.5 * x) + 0.5


def _sigmoid_exp(x):
    return jax.nn.sigmoid(x)


def _norm_kernel(xp_ref, xs_ref, wn_ref, h_ref, *, n_p):
    i = pl.program_id(0)

    def emit(x_ref):
        x = x_ref[...]
        ms = jnp.mean(x * x, axis=-1, keepdims=True)
        h_ref[...] = (x * lax.rsqrt(ms + RMS_EPS) * wn_ref[...]).astype(BF16)

    @pl.when(i < n_p)
    def _():
        emit(xp_ref)

    @pl.when(i >= n_p)
    def _():
        emit(xs_ref)


def _norm_call(xp, xs, wn, *, tm):
    n_p, n_s = xp.shape[0] // tm, xs.shape[0] // tm
    d = xp.shape[1]
    return pl.pallas_call(
        functools.partial(_norm_kernel, n_p=n_p),
        grid=(n_p + n_s,),
        in_specs=[pl.BlockSpec((tm, d), lambda i: (jnp.minimum(i, n_p - 1), 0)),
                  pl.BlockSpec((tm, d), lambda i: (jnp.maximum(i - n_p, 0), 0)),
                  pl.BlockSpec((1, d), lambda i: (0, 0))],
        out_specs=pl.BlockSpec((tm, d), lambda i: (i, 0)),
        out_shape=jax.ShapeDtypeStruct((xp.shape[0] + xs.shape[0], d), BF16),
        compiler_params=pltpu.CompilerParams(dimension_semantics=("arbitrary",)),
        name="rmsnorm_in",
    )(xp, xs, wn)


def _proj_uv_kernel(h_ref, wa_ref, wb_ref, *rest, n_side):
    side_in, (uv_ref, *side_out), (wa_bf, wb_bf) = rest[:n_side], rest[n_side:2 * n_side + 1], rest[2 * n_side + 1:]
    j = pl.program_id(0)
    i = pl.program_id(1)

    @pl.when(i == 0)
    def _():
        wa_bf[...] = wa_ref[...].astype(BF16)

    @pl.when((i == 0) & (j == 1))
    def _():
        wb_bf[...] = wb_ref[...].astype(BF16)

    for src, dst in zip(side_in, side_out):
        dst[...] = src[...].astype(BF16)

    def proj(w_bf):
        return jnp.dot(h_ref[...], w_bf[...], preferred_element_type=F32)

    @pl.when(j == 0)
    def _():
        uv_ref[...] = proj(wa_bf)

    @pl.when(j == 1)
    def _():
        uv_ref[...] = proj(wa_bf) * _sigmoid(proj(wb_bf))


def _proj_uv_call(h, w_in, side_weights, layer, *, tm, pw):
    n, d = h.shape
    n_i = n // tm
    steps = 2 * n_i
    side_specs, side_out_specs, side_shapes = [], [], []
    for w in side_weights:
        rows, cols = w.shape[1:]
        assert rows % steps == 0 and (rows // steps) % (2 * SUBLANES) == 0
        rb = rows // steps
        side_specs.append(pl.BlockSpec((None, rb, cols), lambda j, i: (layer, j * n_i + i, 0)))
        side_out_specs.append(pl.BlockSpec((rb, cols), lambda j, i: (j * n_i + i, 0)))
        side_shapes.append(jax.ShapeDtypeStruct((rows, cols), BF16))
    outs = pl.pallas_call(
        functools.partial(_proj_uv_kernel, n_side=len(side_weights)),
        grid=(2, n_i),
        in_specs=[pl.BlockSpec((tm, d), lambda j, i: (i, 0)),
                  pl.BlockSpec((None, d, pw), lambda j, i: (layer, 0, 2 * j),
                               pipeline_mode=pl.Buffered(1)),
                  pl.BlockSpec((None, d, pw), lambda j, i: (layer, 0, 3),
                               pipeline_mode=pl.Buffered(1))] + side_specs,
        out_specs=[pl.BlockSpec((tm, pw), lambda j, i: (i, j))] + side_out_specs,
        out_shape=[jax.ShapeDtypeStruct((n, 2 * pw), F32)] + side_shapes,
        scratch_shapes=[pltpu.VMEM((d, pw), BF16), pltpu.VMEM((d, pw), BF16)],
        compiler_params=pltpu.CompilerParams(dimension_semantics=("arbitrary", "arbitrary"),
                                             vmem_limit_bytes=VMEM_LIMIT_BYTES),
        name="proj_uv",
    )(h, w_in, w_in, *side_weights)
    return outs[0], outs[1:]


def _conv_block(ext_ref, c, start, wv, bv, prev, never):
    acc = bv if prev is None else jnp.where(never, prev, bv)
    for k, w in enumerate(wv):
        acc = acc + w * ext_ref[c, start + k:start + k + CONV_ROWS, :]
    return acc


def _proj_gate_kernel(h_ref, w_ref, v_ref, cconv_ref, cw_ref, cb_ref, g_ref, cv_ref, w_bf, ext,
                      *, n_full, rem, seg, hpv, conv_k):
    j = pl.program_id(0)
    i = pl.program_id(1)
    tm = h_ref.shape[0]
    n_c = ext.shape[0]
    does_conv = j >= 2
    never = i < 0

    @pl.when(i == 0)
    def _():
        w_bf[...] = w_ref[...].astype(BF16)

    @pl.when(does_conv & (i == 0))
    def _():
        ext[:, 0:hpv, :] = jnp.zeros((n_c, hpv, LANES), F32)

    @pl.when(does_conv & (i > 0))
    def _():
        ext[:, 0:hpv, :] = ext[:, tm:tm + hpv, :]

    def proj():
        return jnp.dot(h_ref[...], w_bf[...], preferred_element_type=F32)

    def conv_rows(c, acc, wv, bv, ext_row0, out_row0, nrows):
        cols = slice(LANES * c, LANES * (c + 1))
        for r in range(0, nrows, CONV_ROWS):
            acc = _conv_block(ext, c, ext_row0 - (conv_k - 1) + r, wv, bv, acc, never)
            cv_ref[out_row0 + r:out_row0 + r + CONV_ROWS, cols] = acc
        return acc

    def conv_tile(n_prompt_rows):
        n_seg = (tm - n_prompt_rows) // seg
        for c in range(n_c):
            cols = slice(LANES * c, LANES * (c + 1))
            wv = [jnp.broadcast_to(cw_ref[k:k + 1, cols], (CONV_ROWS, LANES)) for k in range(conv_k)]
            bv = jnp.broadcast_to(cb_ref[0:1, cols], (CONV_ROWS, LANES))
            acc = None
            if n_prompt_rows:
                ext[c, hpv:hpv + n_prompt_rows, :] = v_ref[0:n_prompt_rows, cols]
                acc = conv_rows(c, acc, wv, bv, hpv, 0, n_prompt_rows)
            for s in range(n_seg):
                base = hpv + n_prompt_rows + s * (hpv + seg)
                row0 = n_prompt_rows + seg * s
                ext[c, base:base + hpv, :] = cconv_ref[s, :, cols]
                ext[c, base + hpv:base + hpv + seg, :] = v_ref[row0:row0 + seg, cols]
                acc = conv_rows(c, acc, wv, bv, base + hpv, row0, seg)

    @pl.when(j < 2)
    def _():
        z = proj()
        g_ref[...] = (z * _sigmoid(z)).astype(BF16)

    @pl.when(does_conv & (i < n_full))
    def _():
        g_ref[...] = _sigmoid_exp(proj()).astype(BF16)
        conv_tile(tm)

    @pl.when(does_conv & (i >= n_full))
    def _():
        g_ref[...] = _sigmoid_exp(proj()).astype(BF16)
        conv_tile(rem)


def _proj_gate_call(h, w_in, uv, cconv, cw, cb, layer, *, tm, pw, n_prompt_rows, seg):
    n, d = h.shape
    n_i = n // tm
    n_full, rem = divmod(n_prompt_rows, tm)
    n_seg = (n - n_prompt_rows) // seg
    hpv = cconv.shape[2]
    conv_k = cw.shape[1]
    cc = CONV_CHUNKS_PER_JOB * LANES
    n_conv_jobs = pw // cc
    n_j = (2 * pw + 2 * d) // pw
    assert n_conv_jobs == n_j - 2 and n_full + 1 == n_i and rem + n_seg * seg == tm and n_seg == cconv.shape[1]
    assert rem % CONV_ROWS == 0 and seg % CONV_ROWS == 0 and tm % CONV_ROWS == 0 and seg >= hpv

    cjob = lambda j: jnp.maximum(j - 2, 0)
    w_map = lambda j, i: (layer, 0, jnp.where(j == 0, 1, jnp.where(j == 1, 4, j + 3)))
    return pl.pallas_call(
        functools.partial(_proj_gate_kernel, n_full=n_full, rem=rem, seg=seg, hpv=hpv, conv_k=conv_k),
        grid=(n_j, n_i),
        in_specs=[pl.BlockSpec((tm, d), lambda j, i: (i, 0)),
                  pl.BlockSpec((None, d, pw), w_map),
                  pl.BlockSpec((tm, cc), lambda j, i: (jnp.where(j >= 2, i, 0), pw // cc + cjob(j))),
                  pl.BlockSpec((None,) + cconv.shape[1:3] + (cc,), lambda j, i: (layer, 0, 0, cjob(j))),
                  pl.BlockSpec((None, conv_k, cc), lambda j, i: (layer, 0, cjob(j))),
                  pl.BlockSpec((None, 1, cc), lambda j, i: (layer, 0, cjob(j)))],
        out_specs=[pl.BlockSpec((tm, pw), lambda j, i: (i, j)),
                   pl.BlockSpec((tm, cc), lambda j, i: (jnp.where(j >= 2, i, 0), cjob(j)))],
        out_shape=[jax.ShapeDtypeStruct((n, 2 * pw + 2 * d), BF16),
                   jax.ShapeDtypeStruct((n, pw), F32)],
        scratch_shapes=[pltpu.VMEM((d, pw), BF16),
                        pltpu.VMEM((CONV_CHUNKS_PER_JOB, hpv + tm + n_seg * hpv, LANES), F32)],
        compiler_params=pltpu.CompilerParams(dimension_semantics=("arbitrary", "arbitrary"),
                                             vmem_limit_bytes=VMEM_LIMIT_BYTES),
        name="proj_gate",
    )(h, w_in, uv, cconv, cw, cb)


def _mixer_kernel(*refs, tm, n_p, seg, split_x, last, hpu):
    n_x = 2 if split_x else 1
    u_ref, cv_ref, sgp_ref, sgc_ref, smp_ref, smc_ref = refs[:6]
    x_refs = refs[6:6 + n_x]
    (cpool_ref, wmix_ref, wpo_ref, wco_ref, wout_ref, pscale_ref, lng_ref, lnb_ref,
     wn_ref) = refs[6 + n_x:15 + n_x]
    out_refs = refs[15 + n_x:17 + n_x]
    extu, pooled_scr = refs[17 + n_x:]
    n_chunks = u_ref.shape[1] // LANES
    grp_chunks = n_chunks // len(POOL_WINDOWS)
    i = pl.program_id(0)

    def pool_rows(base, nrows, pos0, out_row0):
        for g, w in enumerate(POOL_WINDOWS):
            for cc in range(grp_chunks):
                c = g * grp_chunks + cc
                cur = extu[c, base + hpu:base + hpu + nrows, :]
                s = cur
                for o in range(1, w):
                    s = s + extu[c, base + hpu - o:base + hpu - o + nrows, :]
                if pos0 is None:
                    mean = s * (1.0 / w)
                else:
                    pos = pos0 + lax.broadcasted_iota(jnp.int32, (nrows, LANES), 0)
                    mean = s / jnp.minimum(w, pos + 1).astype(F32)
                pooled_scr[out_row0:out_row0 + nrows, LANES * c:LANES * (c + 1)] = mean - cur

    def tail(x_ref, o_refs):
        pooled = pooled_scr[...]
        gw = pooled.shape[1] // len(POOL_WINDOWS)
        mixed = jnp.concatenate(
            [jnp.dot(pooled[:, gw * g:gw * (g + 1)].astype(BF16), wmix_ref[g], preferred_element_type=F32)
             for g in range(len(POOL_WINDOWS))], axis=1)
        pool_in = (mixed * pscale_ref[...] * sgp_ref[...].astype(F32)).astype(BF16)
        pool_branch = jnp.dot(pool_in, wpo_ref[...], preferred_element_type=F32)

        cv = cv_ref[...]
        mu = jnp.mean(cv, axis=-1, keepdims=True)
        xc = cv - mu
        var = jnp.mean(xc * xc, axis=-1, keepdims=True)
        ln = xc * lax.rsqrt(var + LN_EPS) * lng_ref[...] + lnb_ref[...]
        conv_in = (ln * _sigmoid(ln) * sgc_ref[...].astype(F32)).astype(BF16)
        conv_branch = jnp.dot(conv_in, wco_ref[...], preferred_element_type=F32)

        merged = (smp_ref[...].astype(F32) * pool_branch + smc_ref[...].astype(F32) * conv_branch).astype(BF16)
        y = x_ref[...] + jnp.dot(merged, wout_ref[...], preferred_element_type=F32)
        ms = jnp.mean(y * y, axis=-1, keepdims=True)
        hn = y * lax.rsqrt(ms + RMS_EPS) * wn_ref[...]
        if last:
            o_refs[0][...] = hn
        else:
            o_refs[0][...] = y
            o_refs[1][...] = hn.astype(BF16)

    @pl.when(i < n_p)
    def _():
        @pl.when(i == 0)
        def _():
            extu[:, 0:hpu, :] = jnp.zeros((n_chunks, hpu, LANES), F32)

        @pl.when(i > 0)
        def _():
            extu[:, 0:hpu, :] = extu[:, tm:tm + hpu, :]

        for c in range(n_chunks):
            extu[c, hpu:hpu + tm, :] = u_ref[:, LANES * c:LANES * (c + 1)]
        pool_rows(0, tm, i * tm, 0)
        tail(x_refs[0], out_refs if not last else out_refs[0:1])

    @pl.when(i >= n_p)
    def _():
        spt = tm // seg
        for s in range(spt):
            q = (i - n_p) * spt + s
            bu = s * (hpu + seg)
            for c in range(n_chunks):
                cols = slice(LANES * c, LANES * (c + 1))
                extu[c, bu:bu + hpu, :] = cpool_ref[q, :, cols]
                extu[c, bu + hpu:bu + hpu + seg, :] = u_ref[seg * s:seg * (s + 1), cols]
            pool_rows(bu, seg, None, seg * s)
        tail(x_refs[-1], out_refs if not last else out_refs[1:2])


def _mixer_call(uv, cv, gates, xs, layer, cpool, wmix, wpo, wco, wout, pscale, lng, lnb, wn,
                *, tm, n_prompt_rows, seg, last):
    n = uv.shape[0]
    pw = uv.shape[1] // 2
    d = wout.shape[1]
    n_p = n_prompt_rows // tm
    n_s = (n - n_prompt_rows) // tm
    split_x = len(xs) == 2
    hpu = cpool.shape[2]
    spt = tm // seg
    assert PAST_LEN >= max(POOL_WINDOWS) - 1 and seg >= hpu and n_prompt_rows % tm == 0 and tm % seg == 0

    def const(a):
        rest = a.shape[1:]
        return pl.BlockSpec((None,) + rest, lambda i: (layer,) + (0,) * len(rest),
                            pipeline_mode=pl.Buffered(1))

    def whole(a):
        return pl.BlockSpec(a.shape, lambda i: (0,) * a.ndim, pipeline_mode=pl.Buffered(1))

    row = lambda c: (lambda i: (i, c))
    if split_x:
        x_specs = [pl.BlockSpec((tm, d), lambda i: (jnp.minimum(i, n_p - 1), 0)),
                   pl.BlockSpec((tm, d), lambda i: (jnp.maximum(i - n_p, 0), 0))]
    else:
        x_specs = [pl.BlockSpec((tm, d), row(0))]
    in_specs = ([pl.BlockSpec((tm, pw), row(0)), pl.BlockSpec((tm, pw), row(0)),
                 pl.BlockSpec((tm, pw), row(0)), pl.BlockSpec((tm, pw), row(1)),
                 pl.BlockSpec((tm, d), row(1)), pl.BlockSpec((tm, d), row(2))]
                + x_specs
                + [const(cpool)] + [whole(a) for a in (wmix, wpo, wco, wout)]
                + [const(a) for a in (pscale, lng, lnb, wn)])
    if last:
        out_specs = [pl.BlockSpec((tm, d), lambda i: (jnp.minimum(i, n_p - 1), 0)),
                     pl.BlockSpec((tm, d), lambda i: (jnp.maximum(i - n_p, 0), 0))]
        out_shape = [jax.ShapeDtypeStruct((n_prompt_rows, d), F32),
                     jax.ShapeDtypeStruct((n - n_prompt_rows, d), F32)]
    else:
        out_specs = [pl.BlockSpec((tm, d), row(0)), pl.BlockSpec((tm, d), row(0))]
        out_shape = [jax.ShapeDtypeStruct((n, d), F32), jax.ShapeDtypeStruct((n, d), BF16)]
    n_chunks = pw // LANES
    rows_u = max(hpu + tm, spt * (hpu + seg))
    kern = functools.partial(_mixer_kernel, tm=tm, n_p=n_p, seg=seg, split_x=split_x, last=last, hpu=hpu)
    return pl.pallas_call(
        kern,
        grid=(n_p + n_s,),
        in_specs=in_specs,
        out_specs=out_specs,
        out_shape=out_shape,
        scratch_shapes=[pltpu.VMEM((n_chunks, rows_u, LANES), F32),
                        pltpu.VMEM((tm, pw), F32)],
        compiler_params=pltpu.CompilerParams(dimension_semantics=("arbitrary",),
                                             vmem_limit_bytes=VMEM_LIMIT_BYTES),
        name="mixer_last" if last else "mixer",
    )(uv, cv, gates, gates, gates, gates, *xs, cpool, wmix, wpo, wco, wout, pscale, lng, lnb, wn)


def kernel(x_prompt, x_sample, cache_pool, cache_conv, w_norm, w_in, w_pool_mix, pool_scale, w_pool_out,
           conv_w, conv_b, ln_g, ln_b, w_conv_out, w_out, w_final_norm):
    b, t, d = x_prompt.shape
    sb, st, _ = x_sample.shape
    depth = w_norm.shape[0]
    pw = pool_scale.shape[1]
    hu, hv = cache_pool.shape[2], cache_conv.shape[2]
    assert b == 1 and conv_w.shape[2] == pw and 2 * pw == d and PROJ_TN == pw
    assert max(POOL_WINDOWS) - 1 == hu and conv_w.shape[1] - 1 == hv
    n_prompt = b * t
    xp = x_prompt.reshape(n_prompt, d)
    xs = x_sample.reshape(sb * st, d)
    hpu, hpv = _round_up(hu, SUBLANES), _round_up(hv, SUBLANES)
    cpool = jnp.pad(cache_pool, ((0, 0), (0, 0), (hpu - hu, 0), (0, 0)))
    cconv = jnp.pad(cache_conv, ((0, 0), (0, 0), (hpv - hv, 0), (0, 0)))
    row2 = lambda a: a.reshape(depth, 1, a.shape[-1])
    pscale, cb, lng, lnb = row2(pool_scale), row2(conv_b), row2(ln_g), row2(ln_b)
    wnext = jnp.concatenate([w_norm[1:], w_final_norm[None]], axis=0).reshape(depth, 1, d)
    gw = w_pool_mix.shape[-1]
    side_weights = (w_pool_mix.reshape(depth, -1, gw), w_pool_out, w_conv_out, w_out)

    h = _norm_call(xp, xs, w_norm[0:1], tm=NORM_TM)
    x_parts = (xp, xs)
    pool_p, conv_p, pool_s, conv_s = [], [], [], []
    outs = None
    for l in range(depth):
        uv, (wmix, wpo, wco, wout) = _proj_uv_call(h, w_in, side_weights, l, tm=PROJ_UV_TM, pw=pw)
        wmix = wmix.reshape(w_pool_mix.shape[1:])
        gates, cv = _proj_gate_call(h, w_in, uv, cconv, conv_w, cb, l, tm=PROJ_GATE_TM, pw=pw,
                                    n_prompt_rows=n_prompt, seg=st)
        last = l == depth - 1
        outs = _mixer_call(uv, cv, gates, x_parts, l, cpool, wmix, wpo, wco, wout, pscale, lng, lnb, wnext,
                           tm=MIXER_TM, n_prompt_rows=n_prompt, seg=st, last=last)
        if not last:
            x_parts = (outs[0],)
            h = outs[1]
        uvs = uv[n_prompt:].reshape(sb, st, 2 * pw)
        pool_p.append(uv[n_prompt - hu:n_prompt, :pw].reshape(b, hu, pw))
        conv_p.append(uv[n_prompt - hv:n_prompt, pw:].reshape(b, hv, pw))
        pool_s.append(uvs[:, st - hu:, :pw])
        conv_s.append(uvs[:, st - hv:, pw:])
    y_prompt = outs[0].reshape(b, t, d)
    y_sample = outs[1].reshape(sb, st, d)
    return (y_prompt, y_sample, jnp.stack(pool_p), jnp.stack(conv_p), jnp.stack(pool_s), jnp.stack(conv_s))
```

```python
import functools

import jax
import jax.numpy as jnp
from jax import lax
from jax.experimental import pallas as pl
from jax.experimental.pallas import tpu as pltpu

PAST_LEN = 2048
POOL_WINDOWS = (2, 4, 8, 16)
RMS_EPS = 1e-6
LN_EPS = 1e-5

LANES = 128
SUBLANES = 8
VMEM_LIMIT_BYTES = 58 * 1024 * 1024

NORM_TM = 512
PROJ_UV_TM = 1088
PROJ_GATE_TM = 1088
PROJ_TN = 1024
CONV_CHUNKS_PER_JOB = 2
CONV_ROWS = SUBLANES
MIXER_TM = 256

BF16 = jnp.bfloat16
F32 = jnp.float32


def _round_up(n, m):
    return (n + m - 1) // m * m


def _sigmoid(x):
    return 0.5 * jnp.tanh(0.5 * x) + 0.5


def _sigmoid_exp(x):
    return jax.nn.sigmoid(x)


def _norm_kernel(xp_ref, xs_ref, wn_ref, h_ref, *, n_p):
    i = pl.program_id(0)

    def emit(x_ref):
        x = x_ref[...]
        ms = jnp.mean(x * x, axis=-1, keepdims=True)
        h_ref[...] = (x * lax.rsqrt(ms + RMS_EPS) * wn_ref[...]).astype(BF16)

    @pl.when(i < n_p)
    def _():
        emit(xp_ref)

    @pl.when(i >= n_p)
    def _():
        emit(xs_ref)


def _norm_call(xp, xs, wn, *, tm):
    n_p, n_s = xp.shape[0] // tm, xs.shape[0] // tm
    d = xp.shape[1]
    return pl.pallas_call(
        functools.partial(_norm_kernel, n_p=n_p),
        grid=(n_p + n_s,),
        in_specs=[pl.BlockSpec((tm, d), lambda i: (jnp.minimum(i, n_p - 1), 0)),
                  pl.BlockSpec((tm, d), lambda i: (jnp.maximum(i - n_p, 0), 0)),
                  pl.BlockSpec((1, d), lambda i: (0, 0))],
        out_specs=pl.BlockSpec((tm, d), lambda i: (i, 0)),
        out_shape=jax.ShapeDtypeStruct((xp.shape[0] + xs.shape[0], d), BF16),
        compiler_params=pltpu.CompilerParams(dimension_semantics=("arbitrary",)),
        name="rmsnorm_in",
    )(xp, xs, wn)


def _proj_uv_kernel(h_ref, wa_ref, wb_ref, *rest, n_side):
    side_in, (uv_ref, *side_out), (wa_bf, wb_bf) = rest[:n_side], rest[n_side:2 * n_side + 1], rest[2 * n_side + 1:]
    j = pl.program_id(0)
    i = pl.program_id(1)

    @pl.when(i == 0)
    def _():
        wa_bf[...] = wa_ref[...].astype(BF16)

    @pl.when((i == 0) & (j == 1))
    def _():
        wb_bf[...] = wb_ref[...].astype(BF16)

    for src, dst in zip(side_in, side_out):
        dst[...] = src[...].astype(BF16)

    def proj(w_bf):
        return jnp.dot(h_ref[...], w_bf[...], preferred_element_type=F32)

    @pl.when(j == 0)
    def _():
        uv_ref[...] = proj(wa_bf)

    @pl.when(j == 1)
    def _():
        uv_ref[...] = proj(wa_bf) * _sigmoid(proj(wb_bf))


def _proj_uv_call(h, w_in, side_weights, layer, *, tm, pw):
    n, d = h.shape
    n_i = n // tm
    steps = 2 * n_i
    side_specs, side_out_specs, side_shapes = [], [], []
    for w in side_weights:
        lead, (rows, cols) = w.shape[1:-2], w.shape[-2:]
        assert rows % steps == 0 and (rows // steps) % (2 * SUBLANES) == 0
        rb = rows // steps
        zeros = (0,) * len(lead)
        side_specs.append(pl.BlockSpec((None,) + lead + (rb, cols),
                                       lambda j, i, zeros=zeros: (layer,) + zeros + (j * n_i + i, 0)))
        side_out_specs.append(pl.BlockSpec(lead + (rb, cols), lambda j, i, zeros=zeros: zeros + (j * n_i + i, 0)))
        side_shapes.append(jax.ShapeDtypeStruct(lead + (rows, cols), BF16))
    outs = pl.pallas_call(
        functools.partial(_proj_uv_kernel, n_side=len(side_weights)),
        grid=(2, n_i),
        in_specs=[pl.BlockSpec((tm, d), lambda j, i: (i, 0)),
                  pl.BlockSpec((None, d, pw), lambda j, i: (layer, 0, 2 * j),
                               pipeline_mode=pl.Buffered(1)),
                  pl.BlockSpec((None, d, pw), lambda j, i: (layer, 0, 3),
                               pipeline_mode=pl.Buffered(1))] + side_specs,
        out_specs=[pl.BlockSpec((tm, pw), lambda j, i: (i, j))] + side_out_specs,
        out_shape=[jax.ShapeDtypeStruct((n, 2 * pw), F32)] + side_shapes,
        scratch_shapes=[pltpu.VMEM((d, pw), BF16), pltpu.VMEM((d, pw), BF16)],
        compiler_params=pltpu.CompilerParams(dimension_semantics=("arbitrary", "arbitrary"),
                                             vmem_limit_bytes=VMEM_LIMIT_BYTES),
        name="proj_uv",
    )(h, w_in, w_in, *side_weights)
    return outs[0], outs[1:]


def _conv_block(ext_ref, c, start, wv, bv, prev, never):
    acc = bv if prev is None else jnp.where(never, prev, bv)
    for k, w in enumerate(wv):
        acc = acc + w * ext_ref[c, start + k:start + k + CONV_ROWS, :]
    return acc


def _proj_gate_kernel(h_ref, w_ref, v_ref, cconv_ref, cw_ref, cb_ref, g_ref, cv_ref, w_bf, ext,
                      *, n_full, rem, seg, hpv, conv_k):
    hv = cconv_ref.shape[1]
    j = pl.program_id(0)
    i = pl.program_id(1)
    tm = h_ref.shape[0]
    n_c = ext.shape[0]
    does_conv = j >= 2
    never = i < 0

    @pl.when(i == 0)
    def _():
        w_bf[...] = w_ref[...].astype(BF16)

    @pl.when(does_conv & (i == 0))
    def _():
        ext[:, 0:hpv, :] = jnp.zeros((n_c, hpv, LANES), F32)

    @pl.when(does_conv & (i > 0))
    def _():
        ext[:, 0:hpv, :] = ext[:, tm:tm + hpv, :]

    def proj():
        return jnp.dot(h_ref[...], w_bf[...], preferred_element_type=F32)

    def conv_rows(c, acc, wv, bv, ext_row0, out_row0, nrows):
        cols = slice(LANES * c, LANES * (c + 1))
        for r in range(0, nrows, CONV_ROWS):
            acc = _conv_block(ext, c, ext_row0 - (conv_k - 1) + r, wv, bv, acc, never)
            cv_ref[out_row0 + r:out_row0 + r + CONV_ROWS, cols] = acc
        return acc

    def conv_tile(n_prompt_rows):
        n_seg = (tm - n_prompt_rows) // seg
        for c in range(n_c):
            cols = slice(LANES * c, LANES * (c + 1))
            wv = [jnp.broadcast_to(cw_ref[k:k + 1, cols], (CONV_ROWS, LANES)) for k in range(conv_k)]
            bv = jnp.broadcast_to(cb_ref[0:1, cols], (CONV_ROWS, LANES))
            acc = None
            if n_prompt_rows:
                ext[c, hpv:hpv + n_prompt_rows, :] = v_ref[0:n_prompt_rows, cols]
                acc = conv_rows(c, acc, wv, bv, hpv, 0, n_prompt_rows)
            for s in range(n_seg):
                base = hpv + n_prompt_rows + s * (hpv + seg)
                row0 = n_prompt_rows + seg * s
                ext[c, base + hpv - hv:base + hpv, :] = cconv_ref[s, :, cols]
                ext[c, base + hpv:base + hpv + seg, :] = v_ref[row0:row0 + seg, cols]
                acc = conv_rows(c, acc, wv, bv, base + hpv, row0, seg)

    @pl.when(j < 2)
    def _():
        z = proj()
        g_ref[...] = (z * _sigmoid(z)).astype(BF16)

    @pl.when(does_conv & (i < n_full))
    def _():
        g_ref[...] = _sigmoid_exp(proj()).astype(BF16)
        conv_tile(tm)

    @pl.when(does_conv & (i >= n_full))
    def _():
        g_ref[...] = _sigmoid_exp(proj()).astype(BF16)
        conv_tile(rem)


def _proj_gate_call(h, w_in, uv, cconv, cw, cb, layer, *, tm, pw, n_prompt_rows, seg):
    n, d = h.shape
    n_i = n // tm
    n_full, rem = divmod(n_prompt_rows, tm)
    n_seg = (n - n_prompt_rows) // seg
    hpv = _round_up(cconv.shape[2], SUBLANES)
    conv_k = cw.shape[1]
    cc = CONV_CHUNKS_PER_JOB * LANES
    n_conv_jobs = pw // cc
    n_j = (2 * pw + 2 * d) // pw
    assert n_conv_jobs == n_j - 2 and n_full + 1 == n_i and rem + n_seg * seg == tm and n_seg == cconv.shape[1]
    assert rem % CONV_ROWS == 0 and seg % CONV_ROWS == 0 and tm % CONV_ROWS == 0 and seg >= hpv

    cjob = lambda j: jnp.maximum(j - 2, 0)
    w_map = lambda j, i: (layer, 0, jnp.where(j == 0, 1, jnp.where(j == 1, 4, j + 3)))
    return pl.pallas_call(
        functools.partial(_proj_gate_kernel, n_full=n_full, rem=rem, seg=seg, hpv=hpv, conv_k=conv_k),
        grid=(n_j, n_i),
        in_specs=[pl.BlockSpec((tm, d), lambda j, i: (i, 0)),
                  pl.BlockSpec((None, d, pw), w_map),
                  pl.BlockSpec((tm, cc), lambda j, i: (jnp.where(j >= 2, i, 0), pw // cc + cjob(j))),
                  pl.BlockSpec((None,) + cconv.shape[1:3] + (cc,), lambda j, i: (layer, 0, 0, cjob(j))),
                  pl.BlockSpec((None, conv_k, cc), lambda j, i: (layer, 0, cjob(j))),
                  pl.BlockSpec((None, 1, cc), lambda j, i: (layer, 0, cjob(j)))],
        out_specs=[pl.BlockSpec((tm, pw), lambda j, i: (i, j)),
                   pl.BlockSpec((tm, cc), lambda j, i: (jnp.where(j >= 2, i, 0), cjob(j)))],
        out_shape=[jax.ShapeDtypeStruct((n, 2 * pw + 2 * d), BF16),
                   jax.ShapeDtypeStruct((n, pw), F32)],
        scratch_shapes=[pltpu.VMEM((d, pw), BF16),
                        pltpu.VMEM((CONV_CHUNKS_PER_JOB, hpv + tm + n_seg * hpv, LANES), F32)],
        compiler_params=pltpu.CompilerParams(dimension_semantics=("arbitrary", "arbitrary"),
                                             vmem_limit_bytes=VMEM_LIMIT_BYTES),
        name="proj_gate",
    )(h, w_in, uv, cconv, cw, cb)


def _mixer_kernel(*refs, tm, n_p, seg, split_x, last, hpu):
    n_x = 2 if split_x else 1
    u_ref, cv_ref, sgp_ref, sgc_ref, smp_ref, smc_ref = refs[:6]
    x_refs = refs[6:6 + n_x]
    (cpool_ref, wmix_ref, wpo_ref, wco_ref, wout_ref, pscale_ref, lng_ref, lnb_ref,
     wn_ref) = refs[6 + n_x:15 + n_x]
    out_refs = refs[15 + n_x:17 + n_x]
    extu, pooled_scr = refs[17 + n_x:]
    n_chunks = u_ref.shape[1] // LANES
    grp_chunks = n_chunks // len(POOL_WINDOWS)
    i = pl.program_id(0)

    def pool_sum(c, w, row0, nrows):
        cur = extu[c, row0:row0 + nrows, :]
        s = cur
        for o in range(1, w):
            s = s + extu[c, row0 - o:row0 - o + nrows, :]
        return s, cur

    def pool_rows(base, nrows, out_row0, first_tile=None):
        head = 0 if first_tile is None else hpu
        if head:
            pos1 = lax.broadcasted_iota(jnp.int32, (head, LANES), 0) + 1
        for g, w in enumerate(POOL_WINDOWS):
            for cc in range(grp_chunks):
                c = g * grp_chunks + cc
                cols = slice(LANES * c, LANES * (c + 1))
                if head:
                    s, cur = pool_sum(c, w, base + hpu, head)
                    cnt = jnp.where(first_tile, jnp.minimum(w, pos1), w).astype(F32)
                    pooled_scr[out_row0:out_row0 + head, cols] = s / cnt - cur
                s, cur = pool_sum(c, w, base + hpu + head, nrows - head)
                pooled_scr[out_row0 + head:out_row0 + nrows, cols] = s * (1.0 / w) - cur

    def tail(x_ref, o_refs):
        pooled = pooled_scr[...]
        gw = pooled.shape[1] // len(POOL_WINDOWS)
        mixed = jnp.concatenate(
            [jnp.dot(pooled[:, gw * g:gw * (g + 1)].astype(BF16), wmix_ref[g], preferred_element_type=F32)
             for g in range(len(POOL_WINDOWS))], axis=1)
        pool_in = (mixed * pscale_ref[...] * sgp_ref[...].astype(F32)).astype(BF16)
        pool_branch = jnp.dot(pool_in, wpo_ref[...], preferred_element_type=F32)

        cv = cv_ref[...]
        mu = jnp.mean(cv, axis=-1, keepdims=True)
        xc = cv - mu
        var = jnp.mean(xc * xc, axis=-1, keepdims=True)
        ln = xc * lax.rsqrt(var + LN_EPS) * lng_ref[...] + lnb_ref[...]
        conv_in = (ln * _sigmoid(ln) * sgc_ref[...].astype(F32)).astype(BF16)
        conv_branch = jnp.dot(conv_in, wco_ref[...], preferred_element_type=F32)

        merged = (smp_ref[...].astype(F32) * pool_branch + smc_ref[...].astype(F32) * conv_branch).astype(BF16)
        y = x_ref[...] + jnp.dot(merged, wout_ref[...], preferred_element_type=F32)
        ms = jnp.mean(y * y, axis=-1, keepdims=True)
        hn = y * lax.rsqrt(ms + RMS_EPS) * wn_ref[...]
        if last:
            o_refs[0][...] = hn
        else:
            o_refs[0][...] = y
            o_refs[1][...] = hn.astype(BF16)

    @pl.when(i < n_p)
    def _():
        @pl.when(i == 0)
        def _():
            extu[:, 0:hpu, :] = jnp.zeros((n_chunks, hpu, LANES), F32)

        @pl.when(i > 0)
        def _():
            extu[:, 0:hpu, :] = extu[:, tm:tm + hpu, :]

        for c in range(n_chunks):
            extu[c, hpu:hpu + tm, :] = u_ref[:, LANES * c:LANES * (c + 1)]
        pool_rows(0, tm, 0, first_tile=i == 0)
        tail(x_refs[0], out_refs if not last else out_refs[0:1])

    @pl.when(i >= n_p)
    def _():
        spt = tm // seg
        for s in range(spt):
            q = (i - n_p) * spt + s
            bu = s * (hpu + seg)
            for c in range(n_chunks):
                cols = slice(LANES * c, LANES * (c + 1))
                extu[c, bu + hpu - cpool_ref.shape[1]:bu + hpu, :] = cpool_ref[q, :, cols]
                extu[c, bu + hpu:bu + hpu + seg, :] = u_ref[seg * s:seg * (s + 1), cols]
            pool_rows(bu, seg, seg * s)
        tail(x_refs[-1], out_refs if not last else out_refs[1:2])


def _mixer_call(uv, cv, gates, xs, layer, cpool, wmix, wpo, wco, wout, pscale, lng, lnb, wn,
                *, tm, n_prompt_rows, seg, last):
    n = uv.shape[0]
    pw = uv.shape[1] // 2
    d = wout.shape[1]
    n_p = n_prompt_rows // tm
    n_s = (n - n_prompt_rows) // tm
    split_x = len(xs) == 2
    hpu = _round_up(cpool.shape[2], SUBLANES)
    spt = tm // seg
    assert PAST_LEN >= max(POOL_WINDOWS) - 1 and seg >= hpu and n_prompt_rows % tm == 0 and tm % seg == 0

    def const(a):
        rest = a.shape[1:]
        return pl.BlockSpec((None,) + rest, lambda i: (layer,) + (0,) * len(rest),
                            pipeline_mode=pl.Buffered(1))

    def whole(a):
        return pl.BlockSpec(a.shape, lambda i: (0,) * a.ndim, pipeline_mode=pl.Buffered(1))

    row = lambda c: (lambda i: (i, c))
    if split_x:
        x_specs = [pl.BlockSpec((tm, d), lambda i: (jnp.minimum(i, n_p - 1), 0)),
                   pl.BlockSpec((tm, d), lambda i: (jnp.maximum(i - n_p, 0), 0))]
    else:
        x_specs = [pl.BlockSpec((tm, d), row(0))]
    in_specs = ([pl.BlockSpec((tm, pw), row(0)), pl.BlockSpec((tm, pw), row(0)),
                 pl.BlockSpec((tm, pw), row(0)), pl.BlockSpec((tm, pw), row(1)),
                 pl.BlockSpec((tm, d), row(1)), pl.BlockSpec((tm, d), row(2))]
                + x_specs
                + [const(cpool)] + [whole(a) for a in (wmix, wpo, wco, wout)]
                + [const(a) for a in (pscale, lng, lnb, wn)])
    if last:
        out_specs = [pl.BlockSpec((tm, d), lambda i: (jnp.minimum(i, n_p - 1), 0)),
                     pl.BlockSpec((tm, d), lambda i: (jnp.maximum(i - n_p, 0), 0))]
        out_shape = [jax.ShapeDtypeStruct((n_prompt_rows, d), F32),
                     jax.ShapeDtypeStruct((n - n_prompt_rows, d), F32)]
    else:
        out_specs = [pl.BlockSpec((tm, d), row(0)), pl.BlockSpec((tm, d), row(0))]
        out_shape = [jax.ShapeDtypeStruct((n, d), F32), jax.ShapeDtypeStruct((n, d), BF16)]
    n_chunks = pw // LANES
    rows_u = max(hpu + tm, spt * (hpu + seg))
    kern = functools.partial(_mixer_kernel, tm=tm, n_p=n_p, seg=seg, split_x=split_x, last=last, hpu=hpu)
    return pl.pallas_call(
        kern,
        grid=(n_p + n_s,),
        in_specs=in_specs,
        out_specs=out_specs,
        out_shape=out_shape,
        scratch_shapes=[pltpu.VMEM((n_chunks, rows_u, LANES), F32),
                        pltpu.VMEM((tm, pw), F32)],
        compiler_params=pltpu.CompilerParams(dimension_semantics=("arbitrary",),
                                             vmem_limit_bytes=VMEM_LIMIT_BYTES),
        name="mixer_last" if last else "mixer",
    )(uv, cv, gates, gates, gates, gates, *xs, cpool, wmix, wpo, wco, wout, pscale, lng, lnb, wn)


def kernel(x_prompt, x_sample, cache_pool, cache_conv, w_norm, w_in, w_pool_mix, pool_scale, w_pool_out,
           conv_w, conv_b, ln_g, ln_b, w_conv_out, w_out, w_final_norm):
    b, t, d = x_prompt.shape
    sb, st, _ = x_sample.shape
    depth = w_norm.shape[0]
    pw = pool_scale.shape[1]
    hu, hv = cache_pool.shape[2], cache_conv.shape[2]
    assert b == 1 and conv_w.shape[2] == pw and 2 * pw == d and PROJ_TN == pw
    assert max(POOL_WINDOWS) - 1 == hu and conv_w.shape[1] - 1 == hv
    n_prompt = b * t
    xp = x_prompt.reshape(n_prompt, d)
    xs = x_sample.reshape(sb * st, d)
    cpool, cconv = cache_pool, cache_conv
    row2 = lambda a: a.reshape(depth, 1, a.shape[-1])
    pscale, cb, lng, lnb = row2(pool_scale), row2(conv_b), row2(ln_g), row2(ln_b)
    wnext = jnp.concatenate([w_norm[1:], w_final_norm[None]], axis=0).reshape(depth, 1, d)
    side_weights = (w_pool_mix, w_pool_out, w_conv_out, w_out)

    h = _norm_call(xp, xs, w_norm[0:1], tm=NORM_TM)
    x_parts = (xp, xs)
    pool_p, conv_p, pool_s, conv_s = [], [], [], []
    outs = None
    for l in range(depth):
        uv, (wmix, wpo, wco, wout) = _proj_uv_call(h, w_in, side_weights, l, tm=PROJ_UV_TM, pw=pw)
        gates, cv = _proj_gate_call(h, w_in, uv, cconv, conv_w, cb, l, tm=PROJ_GATE_TM, pw=pw,
                                    n_prompt_rows=n_prompt, seg=st)
        last = l == depth - 1
        outs = _mixer_call(uv, cv, gates, x_parts, l, cpool, wmix, wpo, wco, wout, pscale, lng, lnb, wnext,
                           tm=MIXER_TM, n_prompt_rows=n_prompt, seg=st, last=last)
        if not last:
            x_parts = (outs[0],)
            h = outs[1]
        uvs = uv[n_prompt:].reshape(sb, st, 2 * pw)
        pool_p.append(uv[n_prompt - hu:n_prompt, :pw].reshape(b, hu, pw))
        conv_p.append(uv[n_prompt - hv:n_prompt, pw:].reshape(b, hv, pw))
        pool_s.append(uvs[:, st - hu:, :pw])
        conv_s.append(uvs[:, st - hv:, pw:])
    y_prompt = outs[0].reshape(b, t, d)
    y_sample = outs[1].reshape(sb, st, d)
    return (y_prompt, y_sample, jnp.stack(pool_p), jnp.stack(conv_p), jnp.stack(pool_s), jnp.stack(conv_s))
```

```python
import functools

import jax
import jax.numpy as jnp
from jax import lax
from jax.experimental import pallas as pl
from jax.experimental.pallas import tpu as pltpu

PAST_LEN = 2048
POOL_WINDOWS = (2, 4, 8, 16)
RMS_EPS = 1e-6
LN_EPS = 1e-5

LANES = 128
SUBLANES = 8
VMEM_LIMIT_BYTES = 58 * 1024 * 1024

NORM_TM = 512
PROJ_UV_TM = 544
PROJ_GATE_TM = 1088
PROJ_TN = 1024
CONV_CHUNKS_PER_JOB = 2
CONV_ROWS = SUBLANES
MIXER_TM = 256

BF16 = jnp.bfloat16
F32 = jnp.float32


def _round_up(n, m):
    return (n + m - 1) // m * m


def _sigmoid(x):
    return 0.5 * jnp.tanh(0.5 * x) + 0.5


def _sigmoid_exp(x):
    return jax.nn.sigmoid(x)


def _norm_kernel(xp_ref, xs_ref, wn_ref, h_ref, *, n_p):
    i = pl.program_id(0)

    def emit(x_ref):
        x = x_ref[...]
        ms = jnp.mean(x * x, axis=-1, keepdims=True)
        h_ref[...] = (x * lax.rsqrt(ms + RMS_EPS) * wn_ref[...]).astype(BF16)

    @pl.when(i < n_p)
    def _():
        emit(xp_ref)

    @pl.when(i >= n_p)
    def _():
        emit(xs_ref)


def _norm_call(xp, xs, wn, *, tm):
    n_p, n_s = xp.shape[0] // tm, xs.shape[0] // tm
    d = xp.shape[1]
    return pl.pallas_call(
        functools.partial(_norm_kernel, n_p=n_p),
        grid=(n_p + n_s,),
        in_specs=[pl.BlockSpec((tm, d), lambda i: (jnp.minimum(i, n_p - 1), 0)),
                  pl.BlockSpec((tm, d), lambda i: (jnp.maximum(i - n_p, 0), 0)),
                  pl.BlockSpec((1, d), lambda i: (0, 0))],
        out_specs=pl.BlockSpec((tm, d), lambda i: (i, 0)),
        out_shape=jax.ShapeDtypeStruct((xp.shape[0] + xs.shape[0], d), BF16),
        compiler_params=pltpu.CompilerParams(dimension_semantics=("arbitrary",)),
        name="rmsnorm_in",
    )(xp, xs, wn)


def _proj_uv_kernel(h_ref, wa_ref, wb_ref, *rest, n_side, rem, seg):
    side_in = rest[:n_side]
    uv_ref, pool_p_ref, conv_p_ref, pool_s_ref, conv_s_ref = rest[n_side:n_side + 5]
    side_out = rest[n_side + 5:2 * n_side + 5]
    wa_bf, wb_bf = rest[2 * n_side + 5:]
    j = pl.program_id(0)
    i = pl.program_id(1)
    last_tile = i == pl.num_programs(1) - 1

    def emit_states(p_ref, s_ref):
        hist = p_ref.shape[1]
        p_ref[0] = uv_ref[rem - hist:rem, :]
        for s in range(s_ref.shape[0]):
            end = rem + seg * (s + 1)
            s_ref[s] = uv_ref[end - hist:end, :]

    @pl.when(i == 0)
    def _():
        wa_bf[...] = wa_ref[...].astype(BF16)

    @pl.when((i == 0) & (j == 1))
    def _():
        wb_bf[...] = wb_ref[...].astype(BF16)

    for src, dst in zip(side_in, side_out):
        dst[...] = src[...].astype(BF16)

    def proj(w_bf):
        return jnp.dot(h_ref[...], w_bf[...], preferred_element_type=F32)

    @pl.when(j == 0)
    def _():
        uv_ref[...] = proj(wa_bf)

    @pl.when(j == 1)
    def _():
        uv_ref[...] = proj(wa_bf) * _sigmoid(proj(wb_bf))

    @pl.when((j == 0) & last_tile)
    def _():
        emit_states(pool_p_ref, pool_s_ref)

    @pl.when((j == 1) & last_tile)
    def _():
        emit_states(conv_p_ref, conv_s_ref)


def _proj_uv_call(h, w_in, side_weights, layer, *, tm, pw, n_prompt_rows, seg, hu, hv):
    n, d = h.shape
    n_i = n // tm
    steps = 2 * n_i
    n_full, rem = divmod(n_prompt_rows, tm)
    n_seg = (n - n_prompt_rows) // seg
    assert n_full + 1 == n_i and rem + n_seg * seg == tm and rem >= max(hu, hv) and seg >= max(hu, hv)
    state_shapes = [(1, hu, pw), (1, hv, pw), (n_seg, hu, pw), (n_seg, hv, pw)]
    side_specs, side_out_specs, side_shapes = [], [], []
    for w in side_weights:
        lead, (rows, cols) = w.shape[1:-2], w.shape[-2:]
        rb = max(rows // steps, 2 * SUBLANES)
        nblk = rows // rb
        assert rows % rb == 0 and steps % nblk == 0
        zeros = (0,) * len(lead)
        blk = lambda j, i, nblk=nblk: (j * n_i + i) * nblk // steps
        side_specs.append(pl.BlockSpec((None,) + lead + (rb, cols),
                                       lambda j, i, zeros=zeros, blk=blk: (layer,) + zeros + (blk(j, i), 0)))
        side_out_specs.append(pl.BlockSpec(lead + (rb, cols),
                                           lambda j, i, zeros=zeros, blk=blk: zeros + (blk(j, i), 0)))
        side_shapes.append(jax.ShapeDtypeStruct(lead + (rows, cols), BF16))
    outs = pl.pallas_call(
        functools.partial(_proj_uv_kernel, n_side=len(side_weights), rem=rem, seg=seg),
        grid=(2, n_i),
        in_specs=[pl.BlockSpec((tm, d), lambda j, i: (i, 0)),
                  pl.BlockSpec((None, d, pw), lambda j, i: (layer, 0, 2 * j),
                               pipeline_mode=pl.Buffered(1)),
                  pl.BlockSpec((None, d, pw), lambda j, i: (layer, 0, 3),
                               pipeline_mode=pl.Buffered(1))] + side_specs,
        out_specs=([pl.BlockSpec((tm, pw), lambda j, i: (i, j))]
                   + [pl.BlockSpec(sh, lambda j, i: (0, 0, 0)) for sh in state_shapes] + side_out_specs),
        out_shape=([jax.ShapeDtypeStruct((n, 2 * pw), F32)]
                   + [jax.ShapeDtypeStruct(sh, F32) for sh in state_shapes] + side_shapes),
        scratch_shapes=[pltpu.VMEM((d, pw), BF16), pltpu.VMEM((d, pw), BF16)],
        compiler_params=pltpu.CompilerParams(dimension_semantics=("arbitrary", "arbitrary"),
                                             vmem_limit_bytes=VMEM_LIMIT_BYTES),
        name="proj_uv",
    )(h, w_in, w_in, *side_weights)
    return outs[0], outs[1:5], outs[5:]


def _conv_block(ext_ref, c, start, wv, bv, prev, never):
    acc = bv if prev is None else jnp.where(never, prev, bv)
    for k, w in enumerate(wv):
        acc = acc + w * ext_ref[c, start + k:start + k + CONV_ROWS, :]
    return acc


def _proj_gate_kernel(h_ref, w_ref, v_ref, cconv_ref, cw_ref, cb_ref, g_ref, cv_ref, w_bf, ext,
                      *, n_full, rem, seg, hpv, conv_k):
    hv = cconv_ref.shape[1]
    j = pl.program_id(0)
    i = pl.program_id(1)
    tm = h_ref.shape[0]
    n_c = ext.shape[0]
    does_conv = j >= 2
    never = i < 0

    @pl.when(i == 0)
    def _():
        w_bf[...] = w_ref[...].astype(BF16)

    @pl.when(does_conv & (i == 0))
    def _():
        ext[:, 0:hpv, :] = jnp.zeros((n_c, hpv, LANES), F32)

    @pl.when(does_conv & (i > 0))
    def _():
        ext[:, 0:hpv, :] = ext[:, tm:tm + hpv, :]

    def proj():
        return jnp.dot(h_ref[...], w_bf[...], preferred_element_type=F32)

    def conv_rows(c, acc, wv, bv, ext_row0, out_row0, nrows):
        cols = slice(LANES * c, LANES * (c + 1))
        for r in range(0, nrows, CONV_ROWS):
            acc = _conv_block(ext, c, ext_row0 - (conv_k - 1) + r, wv, bv, acc, never)
            cv_ref[out_row0 + r:out_row0 + r + CONV_ROWS, cols] = acc
        return acc

    def conv_tile(n_prompt_rows):
        n_seg = (tm - n_prompt_rows) // seg
        for c in range(n_c):
            cols = slice(LANES * c, LANES * (c + 1))
            wv = [jnp.broadcast_to(cw_ref[k:k + 1, cols], (CONV_ROWS, LANES)) for k in range(conv_k)]
            bv = jnp.broadcast_to(cb_ref[0:1, cols], (CONV_ROWS, LANES))
            acc = None
            if n_prompt_rows:
                ext[c, hpv:hpv + n_prompt_rows, :] = v_ref[0:n_prompt_rows, cols]
                acc = conv_rows(c, acc, wv, bv, hpv, 0, n_prompt_rows)
            for s in range(n_seg):
                base = hpv + n_prompt_rows + s * (hpv + seg)
                row0 = n_prompt_rows + seg * s
                ext[c, base + hpv - hv:base + hpv, :] = cconv_ref[s, :, cols]
                ext[c, base + hpv:base + hpv + seg, :] = v_ref[row0:row0 + seg, cols]
                acc = conv_rows(c, acc, wv, bv, base + hpv, row0, seg)

    @pl.when(j < 2)
    def _():
        z = proj()
        g_ref[...] = (z * _sigmoid(z)).astype(BF16)

    @pl.when(does_conv & (i < n_full))
    def _():
        g_ref[...] = _sigmoid_exp(proj()).astype(BF16)
        conv_tile(tm)

    @pl.when(does_conv & (i >= n_full))
    def _():
        g_ref[...] = _sigmoid_exp(proj()).astype(BF16)
        conv_tile(rem)


def _proj_gate_call(h, w_in, uv, cconv, cw, cb, layer, *, tm, pw, n_prompt_rows, seg):
    n, d = h.shape
    n_i = n // tm
    n_full, rem = divmod(n_prompt_rows, tm)
    n_seg = (n - n_prompt_rows) // seg
    hpv = _round_up(cconv.shape[2], SUBLANES)
    conv_k = cw.shape[1]
    cc = CONV_CHUNKS_PER_JOB * LANES
    n_conv_jobs = pw // cc
    n_j = (2 * pw + 2 * d) // pw
    assert n_conv_jobs == n_j - 2 and n_full + 1 == n_i and rem + n_seg * seg == tm and n_seg == cconv.shape[1]
    assert rem % CONV_ROWS == 0 and seg % CONV_ROWS == 0 and tm % CONV_ROWS == 0 and seg >= hpv

    cjob = lambda j: jnp.maximum(j - 2, 0)
    w_map = lambda j, i: (layer, 0, jnp.where(j == 0, 1, jnp.where(j == 1, 4, j + 3)))
    return pl.pallas_call(
        functools.partial(_proj_gate_kernel, n_full=n_full, rem=rem, seg=seg, hpv=hpv, conv_k=conv_k),
        grid=(n_j, n_i),
        in_specs=[pl.BlockSpec((tm, d), lambda j, i: (i, 0)),
                  pl.BlockSpec((None, d, pw), w_map),
                  pl.BlockSpec((tm, cc), lambda j, i: (jnp.where(j >= 2, i, 0), pw // cc + cjob(j))),
                  pl.BlockSpec((None,) + cconv.shape[1:3] + (cc,), lambda j, i: (layer, 0, 0, cjob(j))),
                  pl.BlockSpec((None, conv_k, cc), lambda j, i: (layer, 0, cjob(j))),
                  pl.BlockSpec((None, 1, cc), lambda j, i: (layer, 0, cjob(j)))],
        out_specs=[pl.BlockSpec((tm, pw), lambda j, i: (i, j)),
                   pl.BlockSpec((tm, cc), lambda j, i: (jnp.where(j >= 2, i, 0), cjob(j)))],
        out_shape=[jax.ShapeDtypeStruct((n, 2 * pw + 2 * d), BF16),
                   jax.ShapeDtypeStruct((n, pw), F32)],
        scratch_shapes=[pltpu.VMEM((d, pw), BF16),
                        pltpu.VMEM((CONV_CHUNKS_PER_JOB, hpv + tm + n_seg * hpv, LANES), F32)],
        compiler_params=pltpu.CompilerParams(dimension_semantics=("arbitrary", "arbitrary"),
                                             vmem_limit_bytes=VMEM_LIMIT_BYTES),
        name="proj_gate",
    )(h, w_in, uv, cconv, cw, cb)


def _mixer_kernel(*refs, tm, n_p, seg, split_x, last, hpu):
    n_x = 2 if split_x else 1
    u_ref, cv_ref, sgp_ref, sgc_ref, smp_ref, smc_ref = refs[:6]
    x_refs = refs[6:6 + n_x]
    (cpool_ref, wmix_ref, wpo_ref, wco_ref, wout_ref, pscale_ref, lng_ref, lnb_ref,
     wn_ref) = refs[6 + n_x:15 + n_x]
    out_refs = refs[15 + n_x:17 + n_x]
    extu, pooled_scr = refs[17 + n_x:]
    n_chunks = u_ref.shape[1] // LANES
    grp_chunks = n_chunks // len(POOL_WINDOWS)
    i = pl.program_id(0)

    def pool_sum(c, w, row0, nrows):
        cur = extu[c, row0:row0 + nrows, :]
        s = cur
        for o in range(1, w):
            s = s + extu[c, row0 - o:row0 - o + nrows, :]
        return s, cur

    def pool_rows(base, nrows, out_row0, first_tile=None):
        head = 0 if first_tile is None else hpu
        if head:
            pos1 = lax.broadcasted_iota(jnp.int32, (head, LANES), 0) + 1
        for g, w in enumerate(POOL_WINDOWS):
            for cc in range(grp_chunks):
                c = g * grp_chunks + cc
                cols = slice(LANES * c, LANES * (c + 1))
                if head:
                    s, cur = pool_sum(c, w, base + hpu, head)
                    cnt = jnp.where(first_tile, jnp.minimum(w, pos1), w).astype(F32)
                    pooled_scr[out_row0:out_row0 + head, cols] = s / cnt - cur
                s, cur = pool_sum(c, w, base + hpu + head, nrows - head)
                pooled_scr[out_row0 + head:out_row0 + nrows, cols] = s * (1.0 / w) - cur

    def tail(x_ref, o_refs):
        pooled = pooled_scr[...]
        gw = pooled.shape[1] // len(POOL_WINDOWS)
        mixed = jnp.concatenate(
            [jnp.dot(pooled[:, gw * g:gw * (g + 1)].astype(BF16), wmix_ref[g], preferred_element_type=F32)
             for g in range(len(POOL_WINDOWS))], axis=1)
        pool_in = (mixed * pscale_ref[...] * sgp_ref[...].astype(F32)).astype(BF16)
        pool_branch = jnp.dot(pool_in, wpo_ref[...], preferred_element_type=F32)

        cv = cv_ref[...]
        mu = jnp.mean(cv, axis=-1, keepdims=True)
        xc = cv - mu
        var = jnp.mean(xc * xc, axis=-1, keepdims=True)
        ln = xc * lax.rsqrt(var + LN_EPS) * lng_ref[...] + lnb_ref[...]
        conv_in = (ln * _sigmoid(ln) * sgc_ref[...].astype(F32)).astype(BF16)
        conv_branch = jnp.dot(conv_in, wco_ref[...], preferred_element_type=F32)

        merged = (smp_ref[...].astype(F32) * pool_branch + smc_ref[...].astype(F32) * conv_branch).astype(BF16)
        y = x_ref[...] + jnp.dot(merged, wout_ref[...], preferred_element_type=F32)
        ms = jnp.mean(y * y, axis=-1, keepdims=True)
        hn = y * lax.rsqrt(ms + RMS_EPS) * wn_ref[...]
        if last:
            o_refs[0][...] = hn
        else:
            o_refs[0][...] = y
            o_refs[1][...] = hn.astype(BF16)

    @pl.when(i < n_p)
    def _():
        @pl.when(i == 0)
        def _():
            extu[:, 0:hpu, :] = jnp.zeros((n_chunks, hpu, LANES), F32)

        @pl.when(i > 0)
        def _():
            extu[:, 0:hpu, :] = extu[:, tm:tm + hpu, :]

        for c in range(n_chunks):
            extu[c, hpu:hpu + tm, :] = u_ref[:, LANES * c:LANES * (c + 1)]
        pool_rows(0, tm, 0, first_tile=i == 0)
        tail(x_refs[0], out_refs if not last else out_refs[0:1])

    @pl.when(i >= n_p)
    def _():
        spt = tm // seg
        for s in range(spt):
            q = (i - n_p) * spt + s
            bu = s * (hpu + seg)
            for c in range(n_chunks):
                cols = slice(LANES * c, LANES * (c + 1))
                extu[c, bu + hpu - cpool_ref.shape[1]:bu + hpu, :] = cpool_ref[q, :, cols]
                extu[c, bu + hpu:bu + hpu + seg, :] = u_ref[seg * s:seg * (s + 1), cols]
            pool_rows(bu, seg, seg * s)
        tail(x_refs[-1], out_refs if not last else out_refs[1:2])


def _mixer_call(uv, cv, gates, xs, layer, cpool, wmix, wpo, wco, wout, pscale, lng, lnb, wn,
                *, tm, n_prompt_rows, seg, last):
    n = uv.shape[0]
    pw = uv.shape[1] // 2
    d = wout.shape[1]
    n_p = n_prompt_rows // tm
    n_s = (n - n_prompt_rows) // tm
    split_x = len(xs) == 2
    hpu = _round_up(cpool.shape[2], SUBLANES)
    spt = tm // seg
    assert PAST_LEN >= max(POOL_WINDOWS) - 1 and seg >= hpu and n_prompt_rows % tm == 0 and tm % seg == 0

    def const(a):
        rest = a.shape[1:]
        return pl.BlockSpec((None,) + rest, lambda i: (layer,) + (0,) * len(rest),
                            pipeline_mode=pl.Buffered(1))

    def whole(a):
        return pl.BlockSpec(a.shape, lambda i: (0,) * a.ndim, pipeline_mode=pl.Buffered(1))

    row = lambda c: (lambda i: (i, c))
    if split_x:
        x_specs = [pl.BlockSpec((tm, d), lambda i: (jnp.minimum(i, n_p - 1), 0)),
                   pl.BlockSpec((tm, d), lambda i: (jnp.maximum(i - n_p, 0), 0))]
    else:
        x_specs = [pl.BlockSpec((tm, d), row(0))]
    in_specs = ([pl.BlockSpec((tm, pw), row(0)), pl.BlockSpec((tm, pw), row(0)),
                 pl.BlockSpec((tm, pw), row(0)), pl.BlockSpec((tm, pw), row(1)),
                 pl.BlockSpec((tm, d), row(1)), pl.BlockSpec((tm, d), row(2))]
                + x_specs
                + [const(cpool)] + [whole(a) for a in (wmix, wpo, wco, wout)]
                + [const(a) for a in (pscale, lng, lnb, wn)])
    if last:
        out_specs = [pl.BlockSpec((tm, d), lambda i: (jnp.minimum(i, n_p - 1), 0)),
                     pl.BlockSpec((tm, d), lambda i: (jnp.maximum(i - n_p, 0), 0))]
        out_shape = [jax.ShapeDtypeStruct((n_prompt_rows, d), F32),
                     jax.ShapeDtypeStruct((n - n_prompt_rows, d), F32)]
    else:
        out_specs = [pl.BlockSpec((tm, d), row(0)), pl.BlockSpec((tm, d), row(0))]
        out_shape = [jax.ShapeDtypeStruct((n, d), F32), jax.ShapeDtypeStruct((n, d), BF16)]
    n_chunks = pw // LANES
    rows_u = max(hpu + tm, spt * (hpu + seg))
    kern = functools.partial(_mixer_kernel, tm=tm, n_p=n_p, seg=seg, split_x=split_x, last=last, hpu=hpu)
    return pl.pallas_call(
        kern,
        grid=(n_p + n_s,),
        in_specs=in_specs,
        out_specs=out_specs,
        out_shape=out_shape,
        scratch_shapes=[pltpu.VMEM((n_chunks, rows_u, LANES), F32),
                        pltpu.VMEM((tm, pw), F32)],
        compiler_params=pltpu.CompilerParams(dimension_semantics=("arbitrary",),
                                             vmem_limit_bytes=VMEM_LIMIT_BYTES),
        name="mixer_last" if last else "mixer",
    )(uv, cv, gates, gates, gates, gates, *xs, cpool, wmix, wpo, wco, wout, pscale, lng, lnb, wn)


def kernel(x_prompt, x_sample, cache_pool, cache_conv, w_norm, w_in, w_pool_mix, pool_scale, w_pool_out,
           conv_w, conv_b, ln_g, ln_b, w_conv_out, w_out, w_final_norm):
    b, t, d = x_prompt.shape
    sb, st, _ = x_sample.shape
    depth = w_norm.shape[0]
    pw = pool_scale.shape[1]
    hu, hv = cache_pool.shape[2], cache_conv.shape[2]
    assert b == 1 and conv_w.shape[2] == pw and 2 * pw == d and PROJ_TN == pw
    assert max(POOL_WINDOWS) - 1 == hu and conv_w.shape[1] - 1 == hv
    n_prompt = b * t
    xp = x_prompt.reshape(n_prompt, d)
    xs = x_sample.reshape(sb * st, d)
    cpool, cconv = cache_pool, cache_conv
    row2 = lambda a: a.reshape(depth, 1, a.shape[-1])
    pscale, cb, lng, lnb = row2(pool_scale), row2(conv_b), row2(ln_g), row2(ln_b)
    wnext = jnp.concatenate([w_norm[1:], w_final_norm[None]], axis=0).reshape(depth, 1, d)
    side_weights = (w_pool_mix, w_pool_out, w_conv_out, w_out)

    h = _norm_call(xp, xs, w_norm[0:1], tm=NORM_TM)
    x_parts = (xp, xs)
    new_states = []
    outs = None
    for l in range(depth):
        uv, states, (wmix, wpo, wco, wout) = _proj_uv_call(
            h, w_in, side_weights, l, tm=PROJ_UV_TM, pw=pw, n_prompt_rows=n_prompt, seg=st, hu=hu, hv=hv)
        new_states.append(states)
        gates, cv = _proj_gate_call(h, w_in, uv, cconv, conv_w, cb, l, tm=PROJ_GATE_TM, pw=pw,
                                    n_prompt_rows=n_prompt, seg=st)
        last = l == depth - 1
        outs = _mixer_call(uv, cv, gates, x_parts, l, cpool, wmix, wpo, wco, wout, pscale, lng, lnb, wnext,
                           tm=MIXER_TM, n_prompt_rows=n_prompt, seg=st, last=last)
        if not last:
            x_parts = (outs[0],)
            h = outs[1]
    y_prompt = outs[0].reshape(b, t, d)
    y_sample = outs[1].reshape(sb, st, d)
    return (y_prompt, y_sample) + tuple(jnp.stack(leaf) for leaf in zip(*new_states))
```

```python
import functools

import jax
import jax.numpy as jnp
from jax import lax
from jax.experimental import pallas as pl
from jax.experimental.pallas import tpu as pltpu

PAST_LEN = 2048
POOL_WINDOWS = (2, 4, 8, 16)
RMS_EPS = 1e-6
LN_EPS = 1e-5

LANES = 128
SUBLANES = 8
VMEM_LIMIT_BYTES = 60 * 1024 * 1024

NORM_TM = 512
PROJ_UV_TM = 1088
PROJ_GATE_TM = 1088
PROJ_TN = 1024
CONV_CHUNKS_PER_JOB = 2
CONV_ROWS = SUBLANES
MIXER_TM = 256

BF16 = jnp.bfloat16
F32 = jnp.float32


def _round_up(n, m):
    return (n + m - 1) // m * m


def _sigmoid(x):
    return 0.5 * jnp.tanh(0.5 * x) + 0.5


def _sigmoid_exp(x):
    return jax.nn.sigmoid(x)


def _norm_kernel(xp_ref, xs_ref, wn_ref, h_ref, *, n_p):
    i = pl.program_id(0)

    def emit(x_ref):
        x = x_ref[...]
        ms = jnp.mean(x * x, axis=-1, keepdims=True)
        h_ref[...] = (x * lax.rsqrt(ms + RMS_EPS) * wn_ref[...]).astype(BF16)

    @pl.when(i < n_p)
    def _():
        emit(xp_ref)

    @pl.when(i >= n_p)
    def _():
        emit(xs_ref)


def _norm_call(xp, xs, wn, *, tm):
    n_p, n_s = xp.shape[0] // tm, xs.shape[0] // tm
    d = xp.shape[1]
    return pl.pallas_call(
        functools.partial(_norm_kernel, n_p=n_p),
        grid=(n_p + n_s,),
        in_specs=[pl.BlockSpec((tm, d), lambda i: (jnp.minimum(i, n_p - 1), 0)),
                  pl.BlockSpec((tm, d), lambda i: (jnp.maximum(i - n_p, 0), 0)),
                  pl.BlockSpec((1, d), lambda i: (0, 0))],
        out_specs=pl.BlockSpec((tm, d), lambda i: (i, 0)),
        out_shape=jax.ShapeDtypeStruct((xp.shape[0] + xs.shape[0], d), BF16),
        compiler_params=pltpu.CompilerParams(dimension_semantics=("arbitrary",)),
        name="rmsnorm_in",
    )(xp, xs, wn)


def _proj_uv_kernel(h_ref, wa_ref, wb_ref, *rest, n_side, rem, seg):
    side_in = rest[:n_side]
    uv_ref, pool_p_ref, conv_p_ref, pool_s_ref, conv_s_ref = rest[n_side:n_side + 5]
    side_out = rest[n_side + 5:2 * n_side + 5]
    wa_bf, wb_bf = rest[2 * n_side + 5:]
    j = pl.program_id(0)
    i = pl.program_id(1)
    last_tile = i == pl.num_programs(1) - 1

    def emit_states(p_ref, s_ref):
        hist = p_ref.shape[1]
        p_ref[0] = uv_ref[rem - hist:rem, :]
        for s in range(s_ref.shape[0]):
            end = rem + seg * (s + 1)
            s_ref[s] = uv_ref[end - hist:end, :]

    @pl.when(i == 0)
    def _():
        wa_bf[...] = wa_ref[...].astype(BF16)

    @pl.when((i == 0) & (j == 1))
    def _():
        wb_bf[...] = wb_ref[...].astype(BF16)

    for src, dst in zip(side_in, side_out):
        dst[...] = src[...].astype(BF16)

    def proj(w_bf):
        return jnp.dot(h_ref[...], w_bf[...], preferred_element_type=F32)

    @pl.when(j == 0)
    def _():
        uv_ref[...] = proj(wa_bf)

    @pl.when(j == 1)
    def _():
        uv_ref[...] = proj(wa_bf) * _sigmoid(proj(wb_bf))

    @pl.when((j == 0) & last_tile)
    def _():
        emit_states(pool_p_ref, pool_s_ref)

    @pl.when((j == 1) & last_tile)
    def _():
        emit_states(conv_p_ref, conv_s_ref)


def _proj_uv_call(h, w_in, side_weights, layer, *, tm, pw, n_prompt_rows, seg, hu, hv):
    n, d = h.shape
    n_i = n // tm
    steps = 2 * n_i
    n_full, rem = divmod(n_prompt_rows, tm)
    n_seg = (n - n_prompt_rows) // seg
    assert n_full + 1 == n_i and rem + n_seg * seg == tm and rem >= max(hu, hv) and seg >= max(hu, hv)
    state_shapes = [(1, hu, pw), (1, hv, pw), (n_seg, hu, pw), (n_seg, hv, pw)]
    side_specs, side_out_specs, side_shapes = [], [], []
    for w in side_weights:
        lead, (rows, cols) = w.shape[1:-2], w.shape[-2:]
        rb = max(rows // steps, 2 * SUBLANES)
        nblk = rows // rb
        assert rows % rb == 0 and steps % nblk == 0
        zeros = (0,) * len(lead)
        blk = lambda j, i, nblk=nblk: (j * n_i + i) * nblk // steps
        side_specs.append(pl.BlockSpec((None,) + lead + (rb, cols),
                                       lambda j, i, zeros=zeros, blk=blk: (layer,) + zeros + (blk(j, i), 0)))
        side_out_specs.append(pl.BlockSpec(lead + (rb, cols),
                                           lambda j, i, zeros=zeros, blk=blk: zeros + (blk(j, i), 0)))
        side_shapes.append(jax.ShapeDtypeStruct(lead + (rows, cols), BF16))
    outs = pl.pallas_call(
        functools.partial(_proj_uv_kernel, n_side=len(side_weights), rem=rem, seg=seg),
        grid=(2, n_i),
        in_specs=[pl.BlockSpec((tm, d), lambda j, i: (i, 0)),
                  pl.BlockSpec((None, d, pw), lambda j, i: (layer, 0, 2 * j),
                               pipeline_mode=pl.Buffered(1)),
                  pl.BlockSpec((None, d, pw), lambda j, i: (layer, 0, 3),
                               pipeline_mode=pl.Buffered(1))] + side_specs,
        out_specs=([pl.BlockSpec((tm, pw), lambda j, i: (i, j))]
                   + [pl.BlockSpec(sh, lambda j, i: (0, 0, 0)) for sh in state_shapes] + side_out_specs),
        out_shape=([jax.ShapeDtypeStruct((n, 2 * pw), F32)]
                   + [jax.ShapeDtypeStruct(sh, F32) for sh in state_shapes] + side_shapes),
        scratch_shapes=[pltpu.VMEM((d, pw), BF16), pltpu.VMEM((d, pw), BF16)],
        compiler_params=pltpu.CompilerParams(dimension_semantics=("arbitrary", "arbitrary"),
                                             vmem_limit_bytes=VMEM_LIMIT_BYTES),
        name="proj_uv",
    )(h, w_in, w_in, *side_weights)
    return outs[0], outs[1:5], outs[5:]


def _conv_block(ext_ref, c, start, wv, bv, prev, never):
    acc = bv if prev is None else jnp.where(never, prev, bv)
    for k, w in enumerate(wv):
        acc = acc + w * ext_ref[c, start + k:start + k + CONV_ROWS, :]
    return acc


def _proj_gate_kernel(h_ref, w_ref, v_ref, cconv_ref, cw_ref, cb_ref, g_ref, cv_ref, w_bf, ext,
                      *, n_full, rem, seg, hpv, conv_k):
    hv = cconv_ref.shape[1]
    j = pl.program_id(0)
    i = pl.program_id(1)
    tm = h_ref.shape[0]
    n_c = ext.shape[0]
    does_conv = j >= 2
    never = i < 0

    @pl.when(i == 0)
    def _():
        w_bf[...] = w_ref[...].astype(BF16)

    @pl.when(does_conv & (i == 0))
    def _():
        ext[:, 0:hpv, :] = jnp.zeros((n_c, hpv, LANES), F32)

    @pl.when(does_conv & (i > 0))
    def _():
        ext[:, 0:hpv, :] = ext[:, tm:tm + hpv, :]

    def proj():
        return jnp.dot(h_ref[...], w_bf[...], preferred_element_type=F32)

    def conv_rows(c, acc, wv, bv, ext_row0, out_row0, nrows):
        cols = slice(LANES * c, LANES * (c + 1))
        for r in range(0, nrows, CONV_ROWS):
            acc = _conv_block(ext, c, ext_row0 - (conv_k - 1) + r, wv, bv, acc, never)
            cv_ref[out_row0 + r:out_row0 + r + CONV_ROWS, cols] = acc
        return acc

    def conv_tile(n_prompt_rows):
        n_seg = (tm - n_prompt_rows) // seg
        for c in range(n_c):
            cols = slice(LANES * c, LANES * (c + 1))
            wv = [jnp.broadcast_to(cw_ref[k:k + 1, cols], (CONV_ROWS, LANES)) for k in range(conv_k)]
            bv = jnp.broadcast_to(cb_ref[0:1, cols], (CONV_ROWS, LANES))
            acc = None
            if n_prompt_rows:
                ext[c, hpv:hpv + n_prompt_rows, :] = v_ref[0:n_prompt_rows, cols]
                acc = conv_rows(c, acc, wv, bv, hpv, 0, n_prompt_rows)
            for s in range(n_seg):
                base = hpv + n_prompt_rows + s * (hpv + seg)
                row0 = n_prompt_rows + seg * s
                ext[c, base + hpv - hv:base + hpv, :] = cconv_ref[s, :, cols]
                ext[c, base + hpv:base + hpv + seg, :] = v_ref[row0:row0 + seg, cols]
                acc = conv_rows(c, acc, wv, bv, base + hpv, row0, seg)

    @pl.when(j < 2)
    def _():
        z = proj()
        g_ref[...] = (z * _sigmoid(z)).astype(BF16)

    @pl.when(does_conv & (i < n_full))
    def _():
        g_ref[...] = _sigmoid_exp(proj()).astype(BF16)
        conv_tile(tm)

    @pl.when(does_conv & (i >= n_full))
    def _():
        g_ref[...] = _sigmoid_exp(proj()).astype(BF16)
        conv_tile(rem)


def _proj_gate_call(h, w_in, uv, cconv, cw, cb, layer, *, tm, pw, n_prompt_rows, seg):
    n, d = h.shape
    n_i = n // tm
    n_full, rem = divmod(n_prompt_rows, tm)
    n_seg = (n - n_prompt_rows) // seg
    hpv = _round_up(cconv.shape[2], SUBLANES)
    conv_k = cw.shape[1]
    cc = CONV_CHUNKS_PER_JOB * LANES
    n_conv_jobs = pw // cc
    n_j = (2 * pw + 2 * d) // pw
    assert n_conv_jobs == n_j - 2 and n_full + 1 == n_i and rem + n_seg * seg == tm and n_seg == cconv.shape[1]
    assert rem % CONV_ROWS == 0 and seg % CONV_ROWS == 0 and tm % CONV_ROWS == 0 and seg >= hpv

    cjob = lambda j: jnp.maximum(j - 2, 0)
    w_map = lambda j, i: (layer, 0, jnp.where(j == 0, 1, jnp.where(j == 1, 4, j + 3)))
    return pl.pallas_call(
        functools.partial(_proj_gate_kernel, n_full=n_full, rem=rem, seg=seg, hpv=hpv, conv_k=conv_k),
        grid=(n_j, n_i),
        in_specs=[pl.BlockSpec((tm, d), lambda j, i: (i, 0)),
                  pl.BlockSpec((None, d, pw), w_map),
                  pl.BlockSpec((tm, cc), lambda j, i: (jnp.where(j >= 2, i, 0), pw // cc + cjob(j))),
                  pl.BlockSpec((None,) + cconv.shape[1:3] + (cc,), lambda j, i: (layer, 0, 0, cjob(j))),
                  pl.BlockSpec((None, conv_k, cc), lambda j, i: (layer, 0, cjob(j))),
                  pl.BlockSpec((None, 1, cc), lambda j, i: (layer, 0, cjob(j)))],
        out_specs=[pl.BlockSpec((tm, pw), lambda j, i: (i, j)),
                   pl.BlockSpec((tm, cc), lambda j, i: (jnp.where(j >= 2, i, 0), cjob(j)))],
        out_shape=[jax.ShapeDtypeStruct((n, 2 * pw + 2 * d), BF16),
                   jax.ShapeDtypeStruct((n, pw), F32)],
        scratch_shapes=[pltpu.VMEM((d, pw), BF16),
                        pltpu.VMEM((CONV_CHUNKS_PER_JOB, hpv + tm + n_seg * hpv, LANES), F32)],
        compiler_params=pltpu.CompilerParams(dimension_semantics=("arbitrary", "arbitrary"),
                                             vmem_limit_bytes=VMEM_LIMIT_BYTES),
        name="proj_gate",
    )(h, w_in, uv, cconv, cw, cb)


def _mixer_kernel(*refs, tm, n_p, seg, split_x, last, hpu):
    n_x = 2 if split_x else 1
    u_ref, cv_ref, sgp_ref, sgc_ref, smp_ref, smc_ref = refs[:6]
    x_refs = refs[6:6 + n_x]
    (cpool_ref, wmix_ref, wpo_ref, wco_ref, wout_ref, pscale_ref, lng_ref, lnb_ref,
     wn_ref) = refs[6 + n_x:15 + n_x]
    out_refs = refs[15 + n_x:17 + n_x]
    extu, pooled_scr = refs[17 + n_x:]
    n_chunks = u_ref.shape[1] // LANES
    grp_chunks = n_chunks // len(POOL_WINDOWS)
    i = pl.program_id(0)

    def pool_sum(c, w, row0, nrows):
        cur = extu[c, row0:row0 + nrows, :]
        s = cur
        for o in range(1, w):
            s = s + extu[c, row0 - o:row0 - o + nrows, :]
        return s, cur

    def pool_rows(base, nrows, out_row0, first_tile=None):
        head = 0 if first_tile is None else hpu
        if head:
            pos1 = lax.broadcasted_iota(jnp.int32, (head, LANES), 0) + 1
        for g, w in enumerate(POOL_WINDOWS):
            for cc in range(grp_chunks):
                c = g * grp_chunks + cc
                cols = slice(LANES * c, LANES * (c + 1))
                if head:
                    s, cur = pool_sum(c, w, base + hpu, head)
                    cnt = jnp.where(first_tile, jnp.minimum(w, pos1), w).astype(F32)
                    pooled_scr[out_row0:out_row0 + head, cols] = s / cnt - cur
                s, cur = pool_sum(c, w, base + hpu + head, nrows - head)
                pooled_scr[out_row0 + head:out_row0 + nrows, cols] = s * (1.0 / w) - cur

    def tail(x_ref, o_refs):
        pooled = pooled_scr[...]
        gw = pooled.shape[1] // len(POOL_WINDOWS)
        mixed = jnp.concatenate(
            [jnp.dot(pooled[:, gw * g:gw * (g + 1)].astype(BF16), wmix_ref[g], preferred_element_type=F32)
             for g in range(len(POOL_WINDOWS))], axis=1)
        pool_in = (mixed * pscale_ref[...] * sgp_ref[...].astype(F32)).astype(BF16)
        pool_branch = jnp.dot(pool_in, wpo_ref[...], preferred_element_type=F32)

        cv = cv_ref[...]
        mu = jnp.mean(cv, axis=-1, keepdims=True)
        xc = cv - mu
        var = jnp.mean(xc * xc, axis=-1, keepdims=True)
        ln = xc * lax.rsqrt(var + LN_EPS) * lng_ref[...] + lnb_ref[...]
        conv_in = (ln * _sigmoid(ln) * sgc_ref[...].astype(F32)).astype(BF16)
        conv_branch = jnp.dot(conv_in, wco_ref[...], preferred_element_type=F32)

        merged = (smp_ref[...].astype(F32) * pool_branch + smc_ref[...].astype(F32) * conv_branch).astype(BF16)
        y = x_ref[...] + jnp.dot(merged, wout_ref[...], preferred_element_type=F32)
        ms = jnp.mean(y * y, axis=-1, keepdims=True)
        hn = y * lax.rsqrt(ms + RMS_EPS) * wn_ref[...]
        if last:
            o_refs[0][...] = hn
        else:
            o_refs[0][...] = y
            o_refs[1][...] = hn.astype(BF16)

    @pl.when(i < n_p)
    def _():
        @pl.when(i == 0)
        def _():
            extu[:, 0:hpu, :] = jnp.zeros((n_chunks, hpu, LANES), F32)

        @pl.when(i > 0)
        def _():
            extu[:, 0:hpu, :] = extu[:, tm:tm + hpu, :]

        for c in range(n_chunks):
            extu[c, hpu:hpu + tm, :] = u_ref[:, LANES * c:LANES * (c + 1)]
        pool_rows(0, tm, 0, first_tile=i == 0)
        tail(x_refs[0], out_refs if not last else out_refs[0:1])

    @pl.when(i >= n_p)
    def _():
        spt = tm // seg
        for s in range(spt):
            q = (i - n_p) * spt + s
            bu = s * (hpu + seg)
            for c in range(n_chunks):
                cols = slice(LANES * c, LANES * (c + 1))
                extu[c, bu + hpu - cpool_ref.shape[1]:bu + hpu, :] = cpool_ref[q, :, cols]
                extu[c, bu + hpu:bu + hpu + seg, :] = u_ref[seg * s:seg * (s + 1), cols]
            pool_rows(bu, seg, seg * s)
        tail(x_refs[-1], out_refs if not last else out_refs[1:2])


def _mixer_call(uv, cv, gates, xs, layer, cpool, wmix, wpo, wco, wout, pscale, lng, lnb, wn,
                *, tm, n_prompt_rows, seg, last):
    n = uv.shape[0]
    pw = uv.shape[1] // 2
    d = wout.shape[1]
    n_p = n_prompt_rows // tm
    n_s = (n - n_prompt_rows) // tm
    split_x = len(xs) == 2
    hpu = _round_up(cpool.shape[2], SUBLANES)
    spt = tm // seg
    assert PAST_LEN >= max(POOL_WINDOWS) - 1 and seg >= hpu and n_prompt_rows % tm == 0 and tm % seg == 0

    def const(a):
        rest = a.shape[1:]
        return pl.BlockSpec((None,) + rest, lambda i: (layer,) + (0,) * len(rest),
                            pipeline_mode=pl.Buffered(1))

    def whole(a):
        return pl.BlockSpec(a.shape, lambda i: (0,) * a.ndim, pipeline_mode=pl.Buffered(1))

    row = lambda c: (lambda i: (i, c))
    if split_x:
        x_specs = [pl.BlockSpec((tm, d), lambda i: (jnp.minimum(i, n_p - 1), 0)),
                   pl.BlockSpec((tm, d), lambda i: (jnp.maximum(i - n_p, 0), 0))]
    else:
        x_specs = [pl.BlockSpec((tm, d), row(0))]
    in_specs = ([pl.BlockSpec((tm, pw), row(0)), pl.BlockSpec((tm, pw), row(0)),
                 pl.BlockSpec((tm, pw), row(0)), pl.BlockSpec((tm, pw), row(1)),
                 pl.BlockSpec((tm, d), row(1)), pl.BlockSpec((tm, d), row(2))]
                + x_specs
                + [const(cpool)] + [whole(a) for a in (wmix, wpo, wco, wout)]
                + [const(a) for a in (pscale, lng, lnb, wn)])
    if last:
        out_specs = [pl.BlockSpec((tm, d), lambda i: (jnp.minimum(i, n_p - 1), 0)),
                     pl.BlockSpec((tm, d), lambda i: (jnp.maximum(i - n_p, 0), 0))]
        out_shape = [jax.ShapeDtypeStruct((n_prompt_rows, d), F32),
                     jax.ShapeDtypeStruct((n - n_prompt_rows, d), F32)]
    else:
        out_specs = [pl.BlockSpec((tm, d), row(0)), pl.BlockSpec((tm, d), row(0))]
        out_shape = [jax.ShapeDtypeStruct((n, d), F32), jax.ShapeDtypeStruct((n, d), BF16)]
    n_chunks = pw // LANES
    rows_u = max(hpu + tm, spt * (hpu + seg))
    kern = functools.partial(_mixer_kernel, tm=tm, n_p=n_p, seg=seg, split_x=split_x, last=last, hpu=hpu)
    return pl.pallas_call(
        kern,
        grid=(n_p + n_s,),
        in_specs=in_specs,
        out_specs=out_specs,
        out_shape=out_shape,
        scratch_shapes=[pltpu.VMEM((n_chunks, rows_u, LANES), F32),
                        pltpu.VMEM((tm, pw), F32)],
        compiler_params=pltpu.CompilerParams(dimension_semantics=("arbitrary",),
                                             vmem_limit_bytes=VMEM_LIMIT_BYTES),
        name="mixer_last" if last else "mixer",
    )(uv, cv, gates, gates, gates, gates, *xs, cpool, wmix, wpo, wco, wout, pscale, lng, lnb, wn)


def kernel(x_prompt, x_sample, cache_pool, cache_conv, w_norm, w_in, w_pool_mix, pool_scale, w_pool_out,
           conv_w, conv_b, ln_g, ln_b, w_conv_out, w_out, w_final_norm):
    b, t, d = x_prompt.shape
    sb, st, _ = x_sample.shape
    depth = w_norm.shape[0]
    pw = pool_scale.shape[1]
    hu, hv = cache_pool.shape[2], cache_conv.shape[2]
    assert b == 1 and conv_w.shape[2] == pw and 2 * pw == d and PROJ_TN == pw
    assert max(POOL_WINDOWS) - 1 == hu and conv_w.shape[1] - 1 == hv
    n_prompt = b * t
    xp = x_prompt.reshape(n_prompt, d)
    xs = x_sample.reshape(sb * st, d)
    cpool, cconv = cache_pool, cache_conv
    row2 = lambda a: a.reshape(depth, 1, a.shape[-1])
    pscale, cb, lng, lnb = row2(pool_scale), row2(conv_b), row2(ln_g), row2(ln_b)
    wnext = jnp.concatenate([w_norm[1:], w_final_norm[None]], axis=0).reshape(depth, 1, d)
    side_weights = (w_pool_mix, w_pool_out, w_conv_out, w_out)

    h = _norm_call(xp, xs, w_norm[0:1], tm=NORM_TM)
    x_parts = (xp, xs)
    new_states = []
    outs = None
    for l in range(depth):
        uv, states, (wmix, wpo, wco, wout) = _proj_uv_call(
            h, w_in, side_weights, l, tm=PROJ_UV_TM, pw=pw, n_prompt_rows=n_prompt, seg=st, hu=hu, hv=hv)
        new_states.append(states)
        gates, cv = _proj_gate_call(h, w_in, uv, cconv, conv_w, cb, l, tm=PROJ_GATE_TM, pw=pw,
                                    n_prompt_rows=n_prompt, seg=st)
        last = l == depth - 1
        outs = _mixer_call(uv, cv, gates, x_parts, l, cpool, wmix, wpo, wco, wout, pscale, lng, lnb, wnext,
                           tm=MIXER_TM, n_prompt_rows=n_prompt, seg=st, last=last)
        if not last:
            x_parts = (outs[0],)
            h = outs[1]
    y_prompt = outs[0].reshape(b, t, d)
    y_sample = outs[1].reshape(sb, st, d)
    return (y_prompt, y_sample) + tuple(jnp.stack(leaf) for leaf in zip(*new_states))
```

```python
import functools

import jax
import jax.numpy as jnp
from jax import lax
from jax.experimental import pallas as pl
from jax.experimental.pallas import tpu as pltpu

PAST_LEN = 2048
POOL_WINDOWS = (2, 4, 8, 16)
RMS_EPS = 1e-6
LN_EPS = 1e-5

LANES = 128
SUBLANES = 8
VMEM_LIMIT_BYTES = 60 * 1024 * 1024

NORM_TM = 512
PROJ_UV_TM = 1088
PROJ_GATE_TM = 1088
PROJ_TN = 1024
CONV_CHUNKS_PER_JOB = 2
CONV_ROWS = SUBLANES
MIXER_TM = 256

BF16 = jnp.bfloat16
F32 = jnp.float32


def _round_up(n, m):
    return (n + m - 1) // m * m


def _sigmoid(x):
    return 0.5 * jnp.tanh(0.5 * x) + 0.5


def _sigmoid_exp(x):
    return jax.nn.sigmoid(x)


def _norm_kernel(xp_ref, xs_ref, wn_ref, h_ref, *, n_p):
    i = pl.program_id(0)

    def emit(x_ref):
        x = x_ref[...]
        ms = jnp.mean(x * x, axis=-1, keepdims=True)
        h_ref[...] = (x * lax.rsqrt(ms + RMS_EPS) * wn_ref[...]).astype(BF16)

    @pl.when(i < n_p)
    def _():
        emit(xp_ref)

    @pl.when(i >= n_p)
    def _():
        emit(xs_ref)


def _norm_call(xp, xs, wn, *, tm):
    n_p, n_s = xp.shape[0] // tm, xs.shape[0] // tm
    d = xp.shape[1]
    return pl.pallas_call(
        functools.partial(_norm_kernel, n_p=n_p),
        grid=(n_p + n_s,),
        in_specs=[pl.BlockSpec((tm, d), lambda i: (jnp.minimum(i, n_p - 1), 0)),
                  pl.BlockSpec((tm, d), lambda i: (jnp.maximum(i - n_p, 0), 0)),
                  pl.BlockSpec((1, d), lambda i: (0, 0))],
        out_specs=pl.BlockSpec((tm, d), lambda i: (i, 0)),
        out_shape=jax.ShapeDtypeStruct((xp.shape[0] + xs.shape[0], d), BF16),
        compiler_params=pltpu.CompilerParams(dimension_semantics=("arbitrary",)),
        name="rmsnorm_in",
    )(xp, xs, wn)


def _proj_uv_kernel(h_ref, wa_ref, wb_ref, *rest, n_side, rem, seg):
    side_in = rest[:n_side]
    uv_ref, pool_p_ref, conv_p_ref, pool_s_ref, conv_s_ref = rest[n_side:n_side + 5]
    side_out = rest[n_side + 5:2 * n_side + 5]
    wa_bf, wb_bf = rest[2 * n_side + 5:]
    j = pl.program_id(0)
    i = pl.program_id(1)
    last_tile = i == pl.num_programs(1) - 1

    def emit_states(p_ref, s_ref):
        hist = p_ref.shape[1]
        p_ref[0] = uv_ref[rem - hist:rem, :]
        for s in range(s_ref.shape[0]):
            end = rem + seg * (s + 1)
            s_ref[s] = uv_ref[end - hist:end, :]

    @pl.when(i == 0)
    def _():
        wa_bf[...] = wa_ref[...].astype(BF16)

    @pl.when((i == 0) & (j == 1))
    def _():
        wb_bf[...] = wb_ref[...].astype(BF16)

    for src, dst in zip(side_in, side_out):
        dst[...] = src[...].astype(BF16)

    def proj(w_bf):
        return jnp.dot(h_ref[...], w_bf[...], preferred_element_type=F32)

    @pl.when(j == 0)
    def _():
        uv_ref[...] = proj(wa_bf)

    @pl.when(j == 1)
    def _():
        uv_ref[...] = proj(wa_bf) * _sigmoid(proj(wb_bf))

    @pl.when((j == 0) & last_tile)
    def _():
        emit_states(pool_p_ref, pool_s_ref)

    @pl.when((j == 1) & last_tile)
    def _():
        emit_states(conv_p_ref, conv_s_ref)


def _proj_uv_call(h, w_in, side_weights, layer, *, tm, pw, n_prompt_rows, seg, hu, hv):
    n, d = h.shape
    n_i = n // tm
    steps = 2 * n_i
    n_full, rem = divmod(n_prompt_rows, tm)
    n_seg = (n - n_prompt_rows) // seg
    assert n_full + 1 == n_i and rem + n_seg * seg == tm and rem >= max(hu, hv) and seg >= max(hu, hv)
    state_shapes = [(1, hu, pw), (1, hv, pw), (n_seg, hu, pw), (n_seg, hv, pw)]
    side_specs, side_out_specs, side_shapes = [], [], []
    for w in side_weights:
        lead, (rows, cols) = w.shape[1:-2], w.shape[-2:]
        rb = max(rows // steps, 2 * SUBLANES)
        nblk = rows // rb
        assert rows % rb == 0 and steps % nblk == 0
        zeros = (0,) * len(lead)
        blk = lambda j, i, nblk=nblk: (j * n_i + i) * nblk // steps
        side_specs.append(pl.BlockSpec((None,) + lead + (rb, cols),
                                       lambda j, i, zeros=zeros, blk=blk: (layer,) + zeros + (blk(j, i), 0)))
        side_out_specs.append(pl.BlockSpec(lead + (rb, cols),
                                           lambda j, i, zeros=zeros, blk=blk: zeros + (blk(j, i), 0)))
        side_shapes.append(jax.ShapeDtypeStruct(lead + (rows, cols), BF16))
    outs = pl.pallas_call(
        functools.partial(_proj_uv_kernel, n_side=len(side_weights), rem=rem, seg=seg),
        grid=(2, n_i),
        in_specs=[pl.BlockSpec((tm, d), lambda j, i: (i, 0)),
                  pl.BlockSpec((None, d, pw), lambda j, i: (layer, 0, 2 * j),
                               pipeline_mode=pl.Buffered(1)),
                  pl.BlockSpec((None, d, pw), lambda j, i: (layer, 0, 3),
                               pipeline_mode=pl.Buffered(1))] + side_specs,
        out_specs=([pl.BlockSpec((tm, pw), lambda j, i: (i, j))]
                   + [pl.BlockSpec(sh, lambda j, i: (0, 0, 0)) for sh in state_shapes] + side_out_specs),
        out_shape=([jax.ShapeDtypeStruct((n, 2 * pw), F32)]
                   + [jax.ShapeDtypeStruct(sh, F32) for sh in state_shapes] + side_shapes),
        scratch_shapes=[pltpu.VMEM((d, pw), BF16), pltpu.VMEM((d, pw), BF16)],
        compiler_params=pltpu.CompilerParams(dimension_semantics=("arbitrary", "arbitrary"),
                                             vmem_limit_bytes=VMEM_LIMIT_BYTES),
        name="proj_uv",
    )(h, w_in, w_in, *side_weights)
    return outs[0], outs[1:5], outs[5:]


def _conv_block(ext_ref, c, start, wv, bv, prev, never):
    acc = bv if prev is None else jnp.where(never, prev, bv)
    for k, w in enumerate(wv):
        acc = acc + w * ext_ref[c, start + k:start + k + CONV_ROWS, :]
    return acc


def _proj_gate_kernel(h_ref, w_ref, v_ref, cconv_ref, cw_ref, cb_ref, g_ref, cv_ref, w_bf, ext,
                      *, n_full, rem, seg, hpv, conv_k):
    hv = cconv_ref.shape[1]
    j = pl.program_id(0)
    i = pl.program_id(1)
    tm = h_ref.shape[0]
    n_c = ext.shape[0]
    does_conv = j >= 2
    never = i < 0

    @pl.when(i == 0)
    def _():
        w_bf[...] = w_ref[...].astype(BF16)

    @pl.when(does_conv & (i == 0))
    def _():
        ext[:, 0:hpv, :] = jnp.zeros((n_c, hpv, LANES), F32)

    @pl.when(does_conv & (i > 0))
    def _():
        ext[:, 0:hpv, :] = ext[:, tm:tm + hpv, :]

    def proj():
        return jnp.dot(h_ref[...], w_bf[...], preferred_element_type=F32)

    def conv_rows(c, acc, wv, bv, ext_row0, out_row0, nrows):
        cols = slice(LANES * c, LANES * (c + 1))
        for r in range(0, nrows, CONV_ROWS):
            acc = _conv_block(ext, c, ext_row0 - (conv_k - 1) + r, wv, bv, acc, never)
            cv_ref[out_row0 + r:out_row0 + r + CONV_ROWS, cols] = acc
        return acc

    def conv_tile(n_prompt_rows):
        n_seg = (tm - n_prompt_rows) // seg
        for c in range(n_c):
            cols = slice(LANES * c, LANES * (c + 1))
            wv = [jnp.broadcast_to(cw_ref[k:k + 1, cols], (CONV_ROWS, LANES)) for k in range(conv_k)]
            bv = jnp.broadcast_to(cb_ref[0:1, cols], (CONV_ROWS, LANES))
            acc = None
            if n_prompt_rows:
                ext[c, hpv:hpv + n_prompt_rows, :] = v_ref[0:n_prompt_rows, cols]
                acc = conv_rows(c, acc, wv, bv, hpv, 0, n_prompt_rows)
            for s in range(n_seg):
                base = hpv + n_prompt_rows + s * (hpv + seg)
                row0 = n_prompt_rows + seg * s
                ext[c, base + hpv - hv:base + hpv, :] = cconv_ref[s, :, cols]
                ext[c, base + hpv:base + hpv + seg, :] = v_ref[row0:row0 + seg, cols]
                acc = conv_rows(c, acc, wv, bv, base + hpv, row0, seg)

    @pl.when(j < 2)
    def _():
        hz = proj().astype(BF16) * 0.5
        g_ref[...] = hz * jnp.tanh(hz) + hz

    @pl.when(does_conv & (i < n_full))
    def _():
        g_ref[...] = jnp.tanh(proj().astype(BF16) * 0.5) * 0.5 + 0.5
        conv_tile(tm)

    @pl.when(does_conv & (i >= n_full))
    def _():
        g_ref[...] = jnp.tanh(proj().astype(BF16) * 0.5) * 0.5 + 0.5
        conv_tile(rem)


def _proj_gate_call(h, w_in, uv, cconv, cw, cb, layer, *, tm, pw, n_prompt_rows, seg):
    n, d = h.shape
    n_i = n // tm
    n_full, rem = divmod(n_prompt_rows, tm)
    n_seg = (n - n_prompt_rows) // seg
    hpv = _round_up(cconv.shape[2], SUBLANES)
    conv_k = cw.shape[1]
    cc = CONV_CHUNKS_PER_JOB * LANES
    n_conv_jobs = pw // cc
    n_j = (2 * pw + 2 * d) // pw
    assert n_conv_jobs == n_j - 2 and n_full + 1 == n_i and rem + n_seg * seg == tm and n_seg == cconv.shape[1]
    assert rem % CONV_ROWS == 0 and seg % CONV_ROWS == 0 and tm % CONV_ROWS == 0 and seg >= hpv

    cjob = lambda j: jnp.maximum(j - 2, 0)
    w_map = lambda j, i: (layer, 0, jnp.where(j == 0, 1, jnp.where(j == 1, 4, j + 3)))
    return pl.pallas_call(
        functools.partial(_proj_gate_kernel, n_full=n_full, rem=rem, seg=seg, hpv=hpv, conv_k=conv_k),
        grid=(n_j, n_i),
        in_specs=[pl.BlockSpec((tm, d), lambda j, i: (i, 0)),
                  pl.BlockSpec((None, d, pw), w_map),
                  pl.BlockSpec((tm, cc), lambda j, i: (jnp.where(j >= 2, i, 0), pw // cc + cjob(j))),
                  pl.BlockSpec((None,) + cconv.shape[1:3] + (cc,), lambda j, i: (layer, 0, 0, cjob(j))),
                  pl.BlockSpec((None, conv_k, cc), lambda j, i: (layer, 0, cjob(j))),
                  pl.BlockSpec((None, 1, cc), lambda j, i: (layer, 0, cjob(j)))],
        out_specs=[pl.BlockSpec((tm, pw), lambda j, i: (i, j)),
                   pl.BlockSpec((tm, cc), lambda j, i: (jnp.where(j >= 2, i, 0), cjob(j)))],
        out_shape=[jax.ShapeDtypeStruct((n, 2 * pw + 2 * d), BF16),
                   jax.ShapeDtypeStruct((n, pw), F32)],
        scratch_shapes=[pltpu.VMEM((d, pw), BF16),
                        pltpu.VMEM((CONV_CHUNKS_PER_JOB, hpv + tm + n_seg * hpv, LANES), F32)],
        compiler_params=pltpu.CompilerParams(dimension_semantics=("arbitrary", "arbitrary"),
                                             vmem_limit_bytes=VMEM_LIMIT_BYTES),
        name="proj_gate",
    )(h, w_in, uv, cconv, cw, cb)


def _mixer_kernel(*refs, tm, n_p, seg, split_x, last, hpu):
    n_x = 2 if split_x else 1
    u_ref, cv_ref, sgp_ref, sgc_ref, smp_ref, smc_ref = refs[:6]
    x_refs = refs[6:6 + n_x]
    (cpool_ref, wmix_ref, wpo_ref, wco_ref, wout_ref, pscale_ref, lng_ref, lnb_ref,
     wn_ref) = refs[6 + n_x:15 + n_x]
    out_refs = refs[15 + n_x:17 + n_x]
    extu, pooled_scr = refs[17 + n_x:]
    n_chunks = u_ref.shape[1] // LANES
    grp_chunks = n_chunks // len(POOL_WINDOWS)
    i = pl.program_id(0)

    def pool_sum(c, w, row0, nrows):
        cur = extu[c, row0:row0 + nrows, :]
        s = cur
        for o in range(1, w):
            s = s + extu[c, row0 - o:row0 - o + nrows, :]
        return s, cur

    def pool_rows(base, nrows, out_row0, first_tile=None):
        head = 0 if first_tile is None else hpu
        if head:
            pos1 = lax.broadcasted_iota(jnp.int32, (head, LANES), 0) + 1
        for g, w in enumerate(POOL_WINDOWS):
            for cc in range(grp_chunks):
                c = g * grp_chunks + cc
                cols = slice(LANES * c, LANES * (c + 1))
                if head:
                    s, cur = pool_sum(c, w, base + hpu, head)
                    cnt = jnp.where(first_tile, jnp.minimum(w, pos1), w).astype(F32)
                    pooled_scr[out_row0:out_row0 + head, cols] = s / cnt - cur
                s, cur = pool_sum(c, w, base + hpu + head, nrows - head)
                pooled_scr[out_row0 + head:out_row0 + nrows, cols] = s * (1.0 / w) - cur

    def tail(x_ref, o_refs):
        pooled = pooled_scr[...]
        gw = pooled.shape[1] // len(POOL_WINDOWS)
        mixed = jnp.concatenate(
            [jnp.dot(pooled[:, gw * g:gw * (g + 1)].astype(BF16), wmix_ref[g], preferred_element_type=F32)
             for g in range(len(POOL_WINDOWS))], axis=1)
        pool_in = (mixed * pscale_ref[...] * sgp_ref[...].astype(F32)).astype(BF16)
        pool_branch = jnp.dot(pool_in, wpo_ref[...], preferred_element_type=F32)

        cv = cv_ref[...]
        mu = jnp.mean(cv, axis=-1, keepdims=True)
        xc = cv - mu
        var = jnp.mean(xc * xc, axis=-1, keepdims=True)
        ln = xc * lax.rsqrt(var + LN_EPS) * lng_ref[...] + lnb_ref[...]
        hl = ln.astype(BF16) * 0.5
        conv_in = (hl * jnp.tanh(hl) + hl) * sgc_ref[...]
        conv_branch = jnp.dot(conv_in, wco_ref[...], preferred_element_type=F32)

        merged = smp_ref[...] * pool_branch.astype(BF16) + smc_ref[...] * conv_branch.astype(BF16)
        y = x_ref[...] + jnp.dot(merged, wout_ref[...], preferred_element_type=F32)
        ms = jnp.mean(y * y, axis=-1, keepdims=True)
        hn = y * lax.rsqrt(ms + RMS_EPS) * wn_ref[...]
        if last:
            o_refs[0][...] = hn
        else:
            o_refs[0][...] = y
            o_refs[1][...] = hn.astype(BF16)

    @pl.when(i < n_p)
    def _():
        @pl.when(i == 0)
        def _():
            extu[:, 0:hpu, :] = jnp.zeros((n_chunks, hpu, LANES), F32)

        @pl.when(i > 0)
        def _():
            extu[:, 0:hpu, :] = extu[:, tm:tm + hpu, :]

        for c in range(n_chunks):
            extu[c, hpu:hpu + tm, :] = u_ref[:, LANES * c:LANES * (c + 1)]
        pool_rows(0, tm, 0, first_tile=i == 0)
        tail(x_refs[0], out_refs if not last else out_refs[0:1])

    @pl.when(i >= n_p)
    def _():
        spt = tm // seg
        for s in range(spt):
            q = (i - n_p) * spt + s
            bu = s * (hpu + seg)
            for c in range(n_chunks):
                cols = slice(LANES * c, LANES * (c + 1))
                extu[c, bu + hpu - cpool_ref.shape[1]:bu + hpu, :] = cpool_ref[q, :, cols]
                extu[c, bu + hpu:bu + hpu + seg, :] = u_ref[seg * s:seg * (s + 1), cols]
            pool_rows(bu, seg, seg * s)
        tail(x_refs[-1], out_refs if not last else out_refs[1:2])


def _mixer_call(uv, cv, gates, xs, layer, cpool, wmix, wpo, wco, wout, pscale, lng, lnb, wn,
                *, tm, n_prompt_rows, seg, last):
    n = uv.shape[0]
    pw = uv.shape[1] // 2
    d = wout.shape[1]
    n_p = n_prompt_rows // tm
    n_s = (n - n_prompt_rows) // tm
    split_x = len(xs) == 2
    hpu = _round_up(cpool.shape[2], SUBLANES)
    spt = tm // seg
    assert PAST_LEN >= max(POOL_WINDOWS) - 1 and seg >= hpu and n_prompt_rows % tm == 0 and tm % seg == 0

    def const(a):
        rest = a.shape[1:]
        return pl.BlockSpec((None,) + rest, lambda i: (layer,) + (0,) * len(rest),
                            pipeline_mode=pl.Buffered(1))

    def whole(a):
        return pl.BlockSpec(a.shape, lambda i: (0,) * a.ndim, pipeline_mode=pl.Buffered(1))

    row = lambda c: (lambda i: (i, c))
    if split_x:
        x_specs = [pl.BlockSpec((tm, d), lambda i: (jnp.minimum(i, n_p - 1), 0)),
                   pl.BlockSpec((tm, d), lambda i: (jnp.maximum(i - n_p, 0), 0))]
    else:
        x_specs = [pl.BlockSpec((tm, d), row(0))]
    in_specs = ([pl.BlockSpec((tm, pw), row(0)), pl.BlockSpec((tm, pw), row(0)),
                 pl.BlockSpec((tm, pw), row(0)), pl.BlockSpec((tm, pw), row(1)),
                 pl.BlockSpec((tm, d), row(1)), pl.BlockSpec((tm, d), row(2))]
                + x_specs
                + [const(cpool)] + [whole(a) for a in (wmix, wpo, wco, wout)]
                + [const(a) for a in (pscale, lng, lnb, wn)])
    if last:
        out_specs = [pl.BlockSpec((tm, d), lambda i: (jnp.minimum(i, n_p - 1), 0)),
                     pl.BlockSpec((tm, d), lambda i: (jnp.maximum(i - n_p, 0), 0))]
        out_shape = [jax.ShapeDtypeStruct((n_prompt_rows, d), F32),
                     jax.ShapeDtypeStruct((n - n_prompt_rows, d), F32)]
    else:
        out_specs = [pl.BlockSpec((tm, d), row(0)), pl.BlockSpec((tm, d), row(0))]
        out_shape = [jax.ShapeDtypeStruct((n, d), F32), jax.ShapeDtypeStruct((n, d), BF16)]
    n_chunks = pw // LANES
    rows_u = max(hpu + tm, spt * (hpu + seg))
    kern = functools.partial(_mixer_kernel, tm=tm, n_p=n_p, seg=seg, split_x=split_x, last=last, hpu=hpu)
    return pl.pallas_call(
        kern,
        grid=(n_p + n_s,),
        in_specs=in_specs,
        out_specs=out_specs,
        out_shape=out_shape,
        scratch_shapes=[pltpu.VMEM((n_chunks, rows_u, LANES), F32),
                        pltpu.VMEM((tm, pw), F32)],
        compiler_params=pltpu.CompilerParams(dimension_semantics=("arbitrary",),
                                             vmem_limit_bytes=VMEM_LIMIT_BYTES),
        name="mixer_last" if last else "mixer",
    )(uv, cv, gates, gates, gates, gates, *xs, cpool, wmix, wpo, wco, wout, pscale, lng, lnb, wn)


def kernel(x_prompt, x_sample, cache_pool, cache_conv, w_norm, w_in, w_pool_mix, pool_scale, w_pool_out,
           conv_w, conv_b, ln_g, ln_b, w_conv_out, w_out, w_final_norm):
    b, t, d = x_prompt.shape
    sb, st, _ = x_sample.shape
    depth = w_norm.shape[0]
    pw = pool_scale.shape[1]
    hu, hv = cache_pool.shape[2], cache_conv.shape[2]
    assert b == 1 and conv_w.shape[2] == pw and 2 * pw == d and PROJ_TN == pw
    assert max(POOL_WINDOWS) - 1 == hu and conv_w.shape[1] - 1 == hv
    n_prompt = b * t
    xp = x_prompt.reshape(n_prompt, d)
    xs = x_sample.reshape(sb * st, d)
    cpool, cconv = cache_pool, cache_conv
    row2 = lambda a: a.reshape(depth, 1, a.shape[-1])
    pscale, cb, lng, lnb = row2(pool_scale), row2(conv_b), row2(ln_g), row2(ln_b)
    wnext = jnp.concatenate([w_norm[1:], w_final_norm[None]], axis=0).reshape(depth, 1, d)
    side_weights = (w_pool_mix, w_pool_out, w_conv_out, w_out)

    h = _norm_call(xp, xs, w_norm[0:1], tm=NORM_TM)
    x_parts = (xp, xs)
    new_states = []
    outs = None
    for l in range(depth):
        uv, states, (wmix, wpo, wco, wout) = _proj_uv_call(
            h, w_in, side_weights, l, tm=PROJ_UV_TM, pw=pw, n_prompt_rows=n_prompt, seg=st, hu=hu, hv=hv)
        new_states.append(states)
        gates, cv = _proj_gate_call(h, w_in, uv, cconv, conv_w, cb, l, tm=PROJ_GATE_TM, pw=pw,
                                    n_prompt_rows=n_prompt, seg=st)
        last = l == depth - 1
        outs = _mixer_call(uv, cv, gates, x_parts, l, cpool, wmix, wpo, wco, wout, pscale, lng, lnb, wnext,
                           tm=MIXER_TM, n_prompt_rows=n_prompt, seg=st, last=last)
        if not last:
            x_parts = (outs[0],)
            h = outs[1]
    y_prompt = outs[0].reshape(b, t, d)
    y_sample = outs[1].reshape(sb, st, d)
    return (y_prompt, y_sample) + tuple(jnp.stack(leaf) for leaf in zip(*new_states))
```

```python
import functools

import jax
import jax.numpy as jnp
from jax import lax
from jax.experimental import pallas as pl
from jax.experimental.pallas import tpu as pltpu

PAST_LEN = 2048
POOL_WINDOWS = (2, 4, 8, 16)
RMS_EPS = 1e-6
LN_EPS = 1e-5

LANES = 128
SUBLANES = 8
VMEM_LIMIT_BYTES = 60 * 1024 * 1024

NORM_TM = 512
PROJ_UV_TM = 1088
PROJ_GATE_TM = 1088
PROJ_TN = 1024
CONV_CHUNKS_PER_JOB = 2
CONV_ROWS = SUBLANES
MIXER_TM = 256

BF16 = jnp.bfloat16
F32 = jnp.float32


def _round_up(n, m):
    return (n + m - 1) // m * m


def _sigmoid(x):
    return 0.5 * jnp.tanh(0.5 * x) + 0.5


def _norm_kernel(xp_ref, xs_ref, wn_ref, h_ref, *, n_p):
    i = pl.program_id(0)

    def emit(x_ref):
        x = x_ref[...]
        ms = jnp.mean(x * x, axis=-1, keepdims=True)
        h_ref[...] = (x * lax.rsqrt(ms + RMS_EPS) * wn_ref[...]).astype(BF16)

    @pl.when(i < n_p)
    def _():
        emit(xp_ref)

    @pl.when(i >= n_p)
    def _():
        emit(xs_ref)


def _norm_call(xp, xs, wn, *, tm):
    n_p, n_s = xp.shape[0] // tm, xs.shape[0] // tm
    d = xp.shape[1]
    return pl.pallas_call(
        functools.partial(_norm_kernel, n_p=n_p),
        grid=(n_p + n_s,),
        in_specs=[pl.BlockSpec((tm, d), lambda i: (jnp.minimum(i, n_p - 1), 0)),
                  pl.BlockSpec((tm, d), lambda i: (jnp.maximum(i - n_p, 0), 0)),
                  pl.BlockSpec((1, d), lambda i: (0, 0))],
        out_specs=pl.BlockSpec((tm, d), lambda i: (i, 0)),
        out_shape=jax.ShapeDtypeStruct((xp.shape[0] + xs.shape[0], d), BF16),
        compiler_params=pltpu.CompilerParams(dimension_semantics=("arbitrary",)),
        name="rmsnorm_in",
    )(xp, xs, wn)


def _proj_uv_kernel(h_ref, wa_ref, wb_ref, *rest, n_side, rem, seg):
    side_in = rest[:n_side]
    uv_ref, pool_p_ref, conv_p_ref, pool_s_ref, conv_s_ref = rest[n_side:n_side + 5]
    side_out = rest[n_side + 5:2 * n_side + 5]
    wa_bf, wb_bf = rest[2 * n_side + 5:]
    j = pl.program_id(0)
    i = pl.program_id(1)
    last_tile = i == pl.num_programs(1) - 1

    def emit_states(p_ref, s_ref):
        hist = p_ref.shape[1]
        p_ref[0] = uv_ref[rem - hist:rem, :]
        for s in range(s_ref.shape[0]):
            end = rem + seg * (s + 1)
            s_ref[s] = uv_ref[end - hist:end, :]

    @pl.when(i == 0)
    def _():
        wa_bf[...] = wa_ref[...].astype(BF16)

    @pl.when((i == 0) & (j == 1))
    def _():
        wb_bf[...] = wb_ref[...].astype(BF16)

    for src, dst in zip(side_in, side_out):
        dst[...] = src[...].astype(BF16)

    def proj(w_bf):
        return jnp.dot(h_ref[...], w_bf[...], preferred_element_type=F32)

    @pl.when(j == 0)
    def _():
        uv_ref[...] = proj(wa_bf)

    @pl.when(j == 1)
    def _():
        uv_ref[...] = proj(wa_bf) * _sigmoid(proj(wb_bf))

    @pl.when((j == 0) & last_tile)
    def _():
        emit_states(pool_p_ref, pool_s_ref)

    @pl.when((j == 1) & last_tile)
    def _():
        emit_states(conv_p_ref, conv_s_ref)


def _proj_uv_call(h, w_in, side_weights, layer, *, tm, pw, n_prompt_rows, seg, hu, hv):
    n, d = h.shape
    n_i = n // tm
    steps = 2 * n_i
    n_full, rem = divmod(n_prompt_rows, tm)
    n_seg = (n - n_prompt_rows) // seg
    assert n_full + 1 == n_i and rem + n_seg * seg == tm and rem >= max(hu, hv) and seg >= max(hu, hv)
    state_shapes = [(1, hu, pw), (1, hv, pw), (n_seg, hu, pw), (n_seg, hv, pw)]
    side_specs, side_out_specs, side_shapes = [], [], []
    for w in side_weights:
        lead, (rows, cols) = w.shape[1:-2], w.shape[-2:]
        rb = max(rows // steps, 2 * SUBLANES)
        nblk = rows // rb
        assert rows % rb == 0 and steps % nblk == 0
        zeros = (0,) * len(lead)
        blk = lambda j, i, nblk=nblk: (j * n_i + i) * nblk // steps
        side_specs.append(pl.BlockSpec((None,) + lead + (rb, cols),
                                       lambda j, i, zeros=zeros, blk=blk: (layer,) + zeros + (blk(j, i), 0)))
        side_out_specs.append(pl.BlockSpec(lead + (rb, cols),
                                           lambda j, i, zeros=zeros, blk=blk: zeros + (blk(j, i), 0)))
        side_shapes.append(jax.ShapeDtypeStruct(lead + (rows, cols), BF16))
    outs = pl.pallas_call(
        functools.partial(_proj_uv_kernel, n_side=len(side_weights), rem=rem, seg=seg),
        grid=(2, n_i),
        in_specs=[pl.BlockSpec((tm, d), lambda j, i: (i, 0)),
                  pl.BlockSpec((None, d, pw), lambda j, i: (layer, 0, 2 * j),
                               pipeline_mode=pl.Buffered(1)),
                  pl.BlockSpec((None, d, pw), lambda j, i: (layer, 0, 3),
                               pipeline_mode=pl.Buffered(1))] + side_specs,
        out_specs=([pl.BlockSpec((tm, pw), lambda j, i: (i, j))]
                   + [pl.BlockSpec(sh, lambda j, i: (0, 0, 0)) for sh in state_shapes] + side_out_specs),
        out_shape=([jax.ShapeDtypeStruct((n, 2 * pw), F32)]
                   + [jax.ShapeDtypeStruct(sh, F32) for sh in state_shapes] + side_shapes),
        scratch_shapes=[pltpu.VMEM((d, pw), BF16), pltpu.VMEM((d, pw), BF16)],
        compiler_params=pltpu.CompilerParams(dimension_semantics=("arbitrary", "arbitrary"),
                                             vmem_limit_bytes=VMEM_LIMIT_BYTES),
        name="proj_uv",
    )(h, w_in, w_in, *side_weights)
    return outs[0], outs[1:5], outs[5:]


def _conv_block(ext_ref, c, start, wv, bv, prev, never):
    acc = bv if prev is None else jnp.where(never, prev, bv)
    for k, w in enumerate(wv):
        acc = acc + w * ext_ref[c, start + k:start + k + CONV_ROWS, :]
    return acc


def _proj_gate_kernel(h_ref, w_ref, v_ref, cconv_ref, cw_ref, cb_ref, g_ref, cv_ref, w_bf, ext,
                      *, n_full, rem, seg, hpv, conv_k):
    hv = cconv_ref.shape[1]
    j = pl.program_id(0)
    i = pl.program_id(1)
    tm = h_ref.shape[0]
    n_c = ext.shape[0]
    does_conv = j >= 2
    never = i < 0

    @pl.when(i == 0)
    def _():
        w_bf[...] = (w_ref[...] * 0.5).astype(BF16)

    @pl.when(does_conv & (i == 0))
    def _():
        ext[:, 0:hpv, :] = jnp.zeros((n_c, hpv, LANES), F32)

    @pl.when(does_conv & (i > 0))
    def _():
        ext[:, 0:hpv, :] = ext[:, tm:tm + hpv, :]

    def proj():
        return jnp.dot(h_ref[...], w_bf[...], preferred_element_type=F32)

    def conv_rows(c, acc, wv, bv, ext_row0, out_row0, nrows):
        cols = slice(LANES * c, LANES * (c + 1))
        for r in range(0, nrows, CONV_ROWS):
            acc = _conv_block(ext, c, ext_row0 - (conv_k - 1) + r, wv, bv, acc, never)
            cv_ref[out_row0 + r:out_row0 + r + CONV_ROWS, cols] = acc
        return acc

    def conv_tile(n_prompt_rows):
        n_seg = (tm - n_prompt_rows) // seg
        for c in range(n_c):
            cols = slice(LANES * c, LANES * (c + 1))
            wv = [jnp.broadcast_to(cw_ref[k:k + 1, cols], (CONV_ROWS, LANES)) for k in range(conv_k)]
            bv = jnp.broadcast_to(cb_ref[0:1, cols], (CONV_ROWS, LANES))
            acc = None
            if n_prompt_rows:
                ext[c, hpv:hpv + n_prompt_rows, :] = v_ref[0:n_prompt_rows, cols]
                acc = conv_rows(c, acc, wv, bv, hpv, 0, n_prompt_rows)
            for s in range(n_seg):
                base = hpv + n_prompt_rows + s * (hpv + seg)
                row0 = n_prompt_rows + seg * s
                ext[c, base + hpv - hv:base + hpv, :] = cconv_ref[s, :, cols]
                ext[c, base + hpv:base + hpv + seg, :] = v_ref[row0:row0 + seg, cols]
                acc = conv_rows(c, acc, wv, bv, base + hpv, row0, seg)

    @pl.when(j < 2)
    def _():
        hz = proj().astype(BF16)
        g_ref[...] = hz * jnp.tanh(hz) + hz

    @pl.when(does_conv & (i < n_full))
    def _():
        g_ref[...] = jnp.tanh(proj().astype(BF16)) * 0.5 + 0.5
        conv_tile(tm)

    @pl.when(does_conv & (i >= n_full))
    def _():
        g_ref[...] = jnp.tanh(proj().astype(BF16)) * 0.5 + 0.5
        conv_tile(rem)


def _proj_gate_call(h, w_in, uv, cconv, cw, cb, layer, *, tm, pw, n_prompt_rows, seg):
    n, d = h.shape
    n_i = n // tm
    n_full, rem = divmod(n_prompt_rows, tm)
    n_seg = (n - n_prompt_rows) // seg
    hpv = _round_up(cconv.shape[2], SUBLANES)
    conv_k = cw.shape[1]
    cc = CONV_CHUNKS_PER_JOB * LANES
    n_conv_jobs = pw // cc
    n_j = (2 * pw + 2 * d) // pw
    assert n_conv_jobs == n_j - 2 and n_full + 1 == n_i and rem + n_seg * seg == tm and n_seg == cconv.shape[1]
    assert rem % CONV_ROWS == 0 and seg % CONV_ROWS == 0 and tm % CONV_ROWS == 0 and seg >= hpv

    cjob = lambda j: jnp.maximum(j - 2, 0)
    w_map = lambda j, i: (layer, 0, jnp.where(j == 0, 1, jnp.where(j == 1, 4, j + 3)))
    return pl.pallas_call(
        functools.partial(_proj_gate_kernel, n_full=n_full, rem=rem, seg=seg, hpv=hpv, conv_k=conv_k),
        grid=(n_j, n_i),
        in_specs=[pl.BlockSpec((tm, d), lambda j, i: (i, 0)),
                  pl.BlockSpec((None, d, pw), w_map),
                  pl.BlockSpec((tm, cc), lambda j, i: (jnp.where(j >= 2, i, 0), pw // cc + cjob(j))),
                  pl.BlockSpec((None,) + cconv.shape[1:3] + (cc,), lambda j, i: (layer, 0, 0, cjob(j))),
                  pl.BlockSpec((None, conv_k, cc), lambda j, i: (layer, 0, cjob(j))),
                  pl.BlockSpec((None, 1, cc), lambda j, i: (layer, 0, cjob(j)))],
        out_specs=[pl.BlockSpec((tm, pw), lambda j, i: (i, j)),
                   pl.BlockSpec((tm, cc), lambda j, i: (jnp.where(j >= 2, i, 0), cjob(j)))],
        out_shape=[jax.ShapeDtypeStruct((n, 2 * pw + 2 * d), BF16),
                   jax.ShapeDtypeStruct((n, pw), F32)],
        scratch_shapes=[pltpu.VMEM((d, pw), BF16),
                        pltpu.VMEM((CONV_CHUNKS_PER_JOB, hpv + tm + n_seg * hpv, LANES), F32)],
        compiler_params=pltpu.CompilerParams(dimension_semantics=("arbitrary", "arbitrary"),
                                             vmem_limit_bytes=VMEM_LIMIT_BYTES),
        name="proj_gate",
    )(h, w_in, uv, cconv, cw, cb)


def _mixer_kernel(*refs, tm, n_p, seg, split_x, last, hpu):
    n_x = 2 if split_x else 1
    u_ref, cv_ref, sgp_ref, sgc_ref, smp_ref, smc_ref = refs[:6]
    x_refs = refs[6:6 + n_x]
    (cpool_ref, wmix_ref, wpo_ref, wco_ref, wout_ref, pscale_ref, lng_ref, lnb_ref,
     wn_ref) = refs[6 + n_x:15 + n_x]
    out_refs = refs[15 + n_x:17 + n_x]
    extu, pooled_scr = refs[17 + n_x:]
    n_chunks = u_ref.shape[1] // LANES
    grp_chunks = n_chunks // len(POOL_WINDOWS)
    i = pl.program_id(0)

    def pool_sum(c, w, row0, nrows):
        cur = extu[c, row0:row0 + nrows, :]
        s = cur
        for o in range(1, w):
            s = s + extu[c, row0 - o:row0 - o + nrows, :]
        return s, cur

    def pool_rows(base, nrows, out_row0, first_tile=None):
        head = 0 if first_tile is None else hpu
        if head:
            pos1 = lax.broadcasted_iota(jnp.int32, (head, LANES), 0) + 1
        for g, w in enumerate(POOL_WINDOWS):
            for cc in range(grp_chunks):
                c = g * grp_chunks + cc
                cols = slice(LANES * c, LANES * (c + 1))
                if head:
                    s, cur = pool_sum(c, w, base + hpu, head)
                    cnt = jnp.where(first_tile, jnp.minimum(w, pos1), w).astype(F32)
                    pooled_scr[out_row0:out_row0 + head, cols] = s / cnt - cur
                s, cur = pool_sum(c, w, base + hpu + head, nrows - head)
                pooled_scr[out_row0 + head:out_row0 + nrows, cols] = s * (1.0 / w) - cur

    def tail(x_ref, o_refs):
        pooled = pooled_scr[...]
        gw = pooled.shape[1] // len(POOL_WINDOWS)
        mixed = jnp.concatenate(
            [jnp.dot(pooled[:, gw * g:gw * (g + 1)].astype(BF16), wmix_ref[g], preferred_element_type=F32)
             for g in range(len(POOL_WINDOWS))], axis=1)
        pool_in = (mixed * pscale_ref[...] * sgp_ref[...].astype(F32)).astype(BF16)
        pool_branch = jnp.dot(pool_in, wpo_ref[...], preferred_element_type=F32)

        cv = cv_ref[...]
        mu = jnp.mean(cv, axis=-1, keepdims=True)
        xc = cv - mu
        var = jnp.mean(xc * xc, axis=-1, keepdims=True)
        ln = xc * lax.rsqrt(var + LN_EPS) * lng_ref[...] + lnb_ref[...]
        hl = ln.astype(BF16) * 0.5
        conv_in = (hl * jnp.tanh(hl) + hl) * sgc_ref[...]
        conv_branch = jnp.dot(conv_in, wco_ref[...], preferred_element_type=F32)

        merged = smp_ref[...] * pool_branch.astype(BF16) + smc_ref[...] * conv_branch.astype(BF16)
        y = x_ref[...] + jnp.dot(merged, wout_ref[...], preferred_element_type=F32)
        ms = jnp.mean(y * y, axis=-1, keepdims=True)
        hn = y * lax.rsqrt(ms + RMS_EPS) * wn_ref[...]
        if last:
            o_refs[0][...] = hn
        else:
            o_refs[0][...] = y
            o_refs[1][...] = hn.astype(BF16)

    @pl.when(i < n_p)
    def _():
        @pl.when(i == 0)
        def _():
            extu[:, 0:hpu, :] = jnp.zeros((n_chunks, hpu, LANES), F32)

        @pl.when(i > 0)
        def _():
            extu[:, 0:hpu, :] = extu[:, tm:tm + hpu, :]

        for c in range(n_chunks):
            extu[c, hpu:hpu + tm, :] = u_ref[:, LANES * c:LANES * (c + 1)]
        pool_rows(0, tm, 0, first_tile=i == 0)
        tail(x_refs[0], out_refs if not last else out_refs[0:1])

    @pl.when(i >= n_p)
    def _():
        spt = tm // seg
        for s in range(spt):
            q = (i - n_p) * spt + s
            bu = s * (hpu + seg)
            for c in range(n_chunks):
                cols = slice(LANES * c, LANES * (c + 1))
                extu[c, bu + hpu - cpool_ref.shape[1]:bu + hpu, :] = cpool_ref[q, :, cols]
                extu[c, bu + hpu:bu + hpu + seg, :] = u_ref[seg * s:seg * (s + 1), cols]
            pool_rows(bu, seg, seg * s)
        tail(x_refs[-1], out_refs if not last else out_refs[1:2])


def _mixer_call(uv, cv, gates, xs, layer, cpool, wmix, wpo, wco, wout, pscale, lng, lnb, wn,
                *, tm, n_prompt_rows, seg, last):
    n = uv.shape[0]
    pw = uv.shape[1] // 2
    d = wout.shape[1]
    n_p = n_prompt_rows // tm
    n_s = (n - n_prompt_rows) // tm
    split_x = len(xs) == 2
    hpu = _round_up(cpool.shape[2], SUBLANES)
    spt = tm // seg
    assert PAST_LEN >= max(POOL_WINDOWS) - 1 and seg >= hpu and n_prompt_rows % tm == 0 and tm % seg == 0

    def const(a):
        rest = a.shape[1:]
        return pl.BlockSpec((None,) + rest, lambda i: (layer,) + (0,) * len(rest),
                            pipeline_mode=pl.Buffered(1))

    def whole(a):
        return pl.BlockSpec(a.shape, lambda i: (0,) * a.ndim, pipeline_mode=pl.Buffered(1))

    row = lambda c: (lambda i: (i, c))
    if split_x:
        x_specs = [pl.BlockSpec((tm, d), lambda i: (jnp.minimum(i, n_p - 1), 0)),
                   pl.BlockSpec((tm, d), lambda i: (jnp.maximum(i - n_p, 0), 0))]
    else:
        x_specs = [pl.BlockSpec((tm, d), row(0))]
    in_specs = ([pl.BlockSpec((tm, pw), row(0)), pl.BlockSpec((tm, pw), row(0)),
                 pl.BlockSpec((tm, pw), row(0)), pl.BlockSpec((tm, pw), row(1)),
                 pl.BlockSpec((tm, d), row(1)), pl.BlockSpec((tm, d), row(2))]
                + x_specs
                + [const(cpool)] + [whole(a) for a in (wmix, wpo, wco, wout)]
                + [const(a) for a in (pscale, lng, lnb, wn)])
    if last:
        out_specs = [pl.BlockSpec((tm, d), lambda i: (jnp.minimum(i, n_p - 1), 0)),
                     pl.BlockSpec((tm, d), lambda i: (jnp.maximum(i - n_p, 0), 0))]
        out_shape = [jax.ShapeDtypeStruct((n_prompt_rows, d), F32),
                     jax.ShapeDtypeStruct((n - n_prompt_rows, d), F32)]
    else:
        out_specs = [pl.BlockSpec((tm, d), row(0)), pl.BlockSpec((tm, d), row(0))]
        out_shape = [jax.ShapeDtypeStruct((n, d), F32), jax.ShapeDtypeStruct((n, d), BF16)]
    n_chunks = pw // LANES
    rows_u = max(hpu + tm, spt * (hpu + seg))
    kern = functools.partial(_mixer_kernel, tm=tm, n_p=n_p, seg=seg, split_x=split_x, last=last, hpu=hpu)
    return pl.pallas_call(
        kern,
        grid=(n_p + n_s,),
        in_specs=in_specs,
        out_specs=out_specs,
        out_shape=out_shape,
        scratch_shapes=[pltpu.VMEM((n_chunks, rows_u, LANES), F32),
                        pltpu.VMEM((tm, pw), F32)],
        compiler_params=pltpu.CompilerParams(dimension_semantics=("arbitrary",),
                                             vmem_limit_bytes=VMEM_LIMIT_BYTES),
        name="mixer_last" if last else "mixer",
    )(uv, cv, gates, gates, gates, gates, *xs, cpool, wmix, wpo, wco, wout, pscale, lng, lnb, wn)


def kernel(x_prompt, x_sample, cache_pool, cache_conv, w_norm, w_in, w_pool_mix, pool_scale, w_pool_out,
           conv_w, conv_b, ln_g, ln_b, w_conv_out, w_out, w_final_norm):
    b, t, d = x_prompt.shape
    sb, st, _ = x_sample.shape
    depth = w_norm.shape[0]
    pw = pool_scale.shape[1]
    hu, hv = cache_pool.shape[2], cache_conv.shape[2]
    assert b == 1 and conv_w.shape[2] == pw and 2 * pw == d and PROJ_TN == pw
    assert max(POOL_WINDOWS) - 1 == hu and conv_w.shape[1] - 1 == hv
    n_prompt = b * t
    xp = x_prompt.reshape(n_prompt, d)
    xs = x_sample.reshape(sb * st, d)
    cpool, cconv = cache_pool, cache_conv
    row2 = lambda a: a.reshape(depth, 1, a.shape[-1])
    pscale, cb, lng, lnb = row2(pool_scale), row2(conv_b), row2(ln_g), row2(ln_b)
    wnext = jnp.concatenate([w_norm[1:], w_final_norm[None]], axis=0).reshape(depth, 1, d)
    side_weights = (w_pool_mix, w_pool_out, w_conv_out, w_out)

    h = _norm_call(xp, xs, w_norm[0:1], tm=NORM_TM)
    x_parts = (xp, xs)
    new_states = []
    outs = None
    for l in range(depth):
        uv, states, (wmix, wpo, wco, wout) = _proj_uv_call(
            h, w_in, side_weights, l, tm=PROJ_UV_TM, pw=pw, n_prompt_rows=n_prompt, seg=st, hu=hu, hv=hv)
        new_states.append(states)
        gates, cv = _proj_gate_call(h, w_in, uv, cconv, conv_w, cb, l, tm=PROJ_GATE_TM, pw=pw,
                                    n_prompt_rows=n_prompt, seg=st)
        last = l == depth - 1
        outs = _mixer_call(uv, cv, gates, x_parts, l, cpool, wmix, wpo, wco, wout, pscale, lng, lnb, wnext,
                           tm=MIXER_TM, n_prompt_rows=n_prompt, seg=st, last=last)
        if not last:
            x_parts = (outs[0],)
            h = outs[1]
    y_prompt = outs[0].reshape(b, t, d)
    y_sample = outs[1].reshape(sb, st, d)
    return (y_prompt, y_sample) + tuple(jnp.stack(leaf) for leaf in zip(*new_states))
```

```python
import functools

import jax
import jax.numpy as jnp
from jax import lax
from jax.experimental import pallas as pl
from jax.experimental.pallas import tpu as pltpu

PAST_LEN = 2048
POOL_WINDOWS = (2, 4, 8, 16)
RMS_EPS = 1e-6
LN_EPS = 1e-5

LANES = 128
SUBLANES = 8
VMEM_LIMIT_BYTES = 60 * 1024 * 1024

NORM_TM = 512
PROJ_UV_TM = 1088
PROJ_GATE_TM = 1088
PROJ_TN = 1024
CONV_CHUNKS_PER_JOB = 2
CONV_ROWS = SUBLANES
MIXER_TM = 256

BF16 = jnp.bfloat16
F32 = jnp.float32


def _round_up(n, m):
    return (n + m - 1) // m * m


def _sigmoid(x):
    return 0.5 * jnp.tanh(0.5 * x) + 0.5


def _norm_kernel(xp_ref, xs_ref, wn_ref, h_ref, *, n_p):
    i = pl.program_id(0)

    def emit(x_ref):
        x = x_ref[...]
        ms = jnp.mean(x * x, axis=-1, keepdims=True)
        h_ref[...] = (x * lax.rsqrt(ms + RMS_EPS) * wn_ref[...]).astype(BF16)

    @pl.when(i < n_p)
    def _():
        emit(xp_ref)

    @pl.when(i >= n_p)
    def _():
        emit(xs_ref)


def _norm_call(xp, xs, wn, *, tm):
    n_p, n_s = xp.shape[0] // tm, xs.shape[0] // tm
    d = xp.shape[1]
    return pl.pallas_call(
        functools.partial(_norm_kernel, n_p=n_p),
        grid=(n_p + n_s,),
        in_specs=[pl.BlockSpec((tm, d), lambda i: (jnp.minimum(i, n_p - 1), 0)),
                  pl.BlockSpec((tm, d), lambda i: (jnp.maximum(i - n_p, 0), 0)),
                  pl.BlockSpec((1, d), lambda i: (0, 0))],
        out_specs=pl.BlockSpec((tm, d), lambda i: (i, 0)),
        out_shape=jax.ShapeDtypeStruct((xp.shape[0] + xs.shape[0], d), BF16),
        compiler_params=pltpu.CompilerParams(dimension_semantics=("arbitrary",)),
        name="rmsnorm_in",
    )(xp, xs, wn)


def _proj_uv_kernel(h_ref, wa_ref, wb_ref, *rest, n_side, rem, seg):
    side_in = rest[:n_side]
    uv_ref, pool_p_ref, conv_p_ref, pool_s_ref, conv_s_ref = rest[n_side:n_side + 5]
    side_out = rest[n_side + 5:2 * n_side + 5]
    wa_bf, wb_bf = rest[2 * n_side + 5:]
    j = pl.program_id(0)
    i = pl.program_id(1)
    last_tile = i == pl.num_programs(1) - 1

    def emit_states(p_ref, s_ref):
        hist = p_ref.shape[1]
        p_ref[0] = uv_ref[rem - hist:rem, :]
        for s in range(s_ref.shape[0]):
            end = rem + seg * (s + 1)
            s_ref[s] = uv_ref[end - hist:end, :]

    @pl.when(i == 0)
    def _():
        wa_bf[...] = wa_ref[...].astype(BF16)

    @pl.when((i == 0) & (j == 1))
    def _():
        wb_bf[...] = wb_ref[...].astype(BF16)

    for src, dst in zip(side_in, side_out):
        dst[...] = src[...].astype(BF16)

    def proj(w_bf):
        return jnp.dot(h_ref[...], w_bf[...], preferred_element_type=F32)

    @pl.when(j == 0)
    def _():
        uv_ref[...] = proj(wa_bf)

    @pl.when(j == 1)
    def _():
        uv_ref[...] = proj(wa_bf) * _sigmoid(proj(wb_bf))

    @pl.when((j == 0) & last_tile)
    def _():
        emit_states(pool_p_ref, pool_s_ref)

    @pl.when((j == 1) & last_tile)
    def _():
        emit_states(conv_p_ref, conv_s_ref)


def _proj_uv_call(h, w_in, side_weights, layer, *, tm, pw, n_prompt_rows, seg, hu, hv):
    n, d = h.shape
    n_i = n // tm
    steps = 2 * n_i
    n_full, rem = divmod(n_prompt_rows, tm)
    n_seg = (n - n_prompt_rows) // seg
    assert n_full + 1 == n_i and rem + n_seg * seg == tm and rem >= max(hu, hv) and seg >= max(hu, hv)
    state_shapes = [(1, hu, pw), (1, hv, pw), (n_seg, hu, pw), (n_seg, hv, pw)]
    side_specs, side_out_specs, side_shapes = [], [], []
    for w in side_weights:
        lead, (rows, cols) = w.shape[1:-2], w.shape[-2:]
        rb = max(rows // steps, 2 * SUBLANES)
        nblk = rows // rb
        assert rows % rb == 0 and steps % nblk == 0
        zeros = (0,) * len(lead)
        blk = lambda j, i, nblk=nblk: (j * n_i + i) * nblk // steps
        side_specs.append(pl.BlockSpec((None,) + lead + (rb, cols),
                                       lambda j, i, zeros=zeros, blk=blk: (layer,) + zeros + (blk(j, i), 0)))
        side_out_specs.append(pl.BlockSpec(lead + (rb, cols),
                                           lambda j, i, zeros=zeros, blk=blk: zeros + (blk(j, i), 0)))
        side_shapes.append(jax.ShapeDtypeStruct(lead + (rows, cols), BF16))
    outs = pl.pallas_call(
        functools.partial(_proj_uv_kernel, n_side=len(side_weights), rem=rem, seg=seg),
        grid=(2, n_i),
        in_specs=[pl.BlockSpec((tm, d), lambda j, i: (i, 0)),
                  pl.BlockSpec((None, d, pw), lambda j, i: (layer, 0, 2 * j),
                               pipeline_mode=pl.Buffered(1)),
                  pl.BlockSpec((None, d, pw), lambda j, i: (layer, 0, 3),
                               pipeline_mode=pl.Buffered(1))] + side_specs,
        out_specs=([pl.BlockSpec((tm, pw), lambda j, i: (i, j))]
                   + [pl.BlockSpec(sh, lambda j, i: (0, 0, 0)) for sh in state_shapes] + side_out_specs),
        out_shape=([jax.ShapeDtypeStruct((n, 2 * pw), F32)]
                   + [jax.ShapeDtypeStruct(sh, F32) for sh in state_shapes] + side_shapes),
        scratch_shapes=[pltpu.VMEM((d, pw), BF16), pltpu.VMEM((d, pw), BF16)],
        compiler_params=pltpu.CompilerParams(dimension_semantics=("arbitrary", "arbitrary"),
                                             vmem_limit_bytes=VMEM_LIMIT_BYTES),
        name="proj_uv",
    )(h, w_in, w_in, *side_weights)
    return outs[0], outs[1:5], outs[5:]


def _conv_block(ext_ref, c, start, wv, bv, prev, never):
    acc = bv if prev is None else jnp.where(never, prev, bv)
    for k, w in enumerate(wv):
        acc = acc + w * ext_ref[c, start + k:start + k + CONV_ROWS, :]
    return acc


def _proj_gate_kernel(h_ref, w_ref, v_ref, cconv_ref, cw_ref, cb_ref, g_ref, cv_ref, w_bf, ext,
                      *, n_full, rem, seg, hpv, conv_k):
    hv = cconv_ref.shape[1]
    j = pl.program_id(0)
    i = pl.program_id(1)
    tm = h_ref.shape[0]
    n_c = ext.shape[0]
    does_conv = j >= 2
    never = i < 0

    @pl.when(i == 0)
    def _():
        w_bf[...] = (w_ref[...] * 0.5).astype(BF16)

    @pl.when(does_conv & (i == 0))
    def _():
        ext[:, 0:hpv, :] = jnp.zeros((n_c, hpv, LANES), F32)

    @pl.when(does_conv & (i > 0))
    def _():
        ext[:, 0:hpv, :] = ext[:, tm:tm + hpv, :]

    def proj():
        return jnp.dot(h_ref[...], w_bf[...], preferred_element_type=F32)

    def conv_rows(c, acc, wv, bv, ext_row0, out_row0, nrows):
        cols = slice(LANES * c, LANES * (c + 1))
        for r in range(0, nrows, CONV_ROWS):
            acc = _conv_block(ext, c, ext_row0 - (conv_k - 1) + r, wv, bv, acc, never)
            cv_ref[out_row0 + r:out_row0 + r + CONV_ROWS, cols] = acc
        return acc

    def conv_tile(n_prompt_rows):
        n_seg = (tm - n_prompt_rows) // seg
        for c in range(n_c):
            cols = slice(LANES * c, LANES * (c + 1))
            wv = [jnp.broadcast_to(cw_ref[k:k + 1, cols], (CONV_ROWS, LANES)) for k in range(conv_k)]
            bv = jnp.broadcast_to(cb_ref[0:1, cols], (CONV_ROWS, LANES))
            acc = None
            if n_prompt_rows:
                ext[c, hpv:hpv + n_prompt_rows, :] = v_ref[0:n_prompt_rows, cols]
                acc = conv_rows(c, acc, wv, bv, hpv, 0, n_prompt_rows)
            for s in range(n_seg):
                base = hpv + n_prompt_rows + s * (hpv + seg)
                row0 = n_prompt_rows + seg * s
                ext[c, base + hpv - hv:base + hpv, :] = cconv_ref[s, :, cols]
                ext[c, base + hpv:base + hpv + seg, :] = v_ref[row0:row0 + seg, cols]
                acc = conv_rows(c, acc, wv, bv, base + hpv, row0, seg)

    @pl.when(j < 2)
    def _():
        hz = proj().astype(BF16)
        g_ref[...] = hz * jnp.tanh(hz) + hz

    @pl.when(does_conv & (i < n_full))
    def _():
        g_ref[...] = jnp.tanh(proj().astype(BF16))
        conv_tile(tm)

    @pl.when(does_conv & (i >= n_full))
    def _():
        g_ref[...] = jnp.tanh(proj().astype(BF16))
        conv_tile(rem)


def _proj_gate_call(h, w_in, uv, cconv, cw, cb, layer, *, tm, pw, n_prompt_rows, seg):
    n, d = h.shape
    n_i = n // tm
    n_full, rem = divmod(n_prompt_rows, tm)
    n_seg = (n - n_prompt_rows) // seg
    hpv = _round_up(cconv.shape[2], SUBLANES)
    conv_k = cw.shape[1]
    cc = CONV_CHUNKS_PER_JOB * LANES
    n_conv_jobs = pw // cc
    n_j = (2 * pw + 2 * d) // pw
    assert n_conv_jobs == n_j - 2 and n_full + 1 == n_i and rem + n_seg * seg == tm and n_seg == cconv.shape[1]
    assert rem % CONV_ROWS == 0 and seg % CONV_ROWS == 0 and tm % CONV_ROWS == 0 and seg >= hpv

    cjob = lambda j: jnp.maximum(j - 2, 0)
    w_map = lambda j, i: (layer, 0, jnp.where(j == 0, 1, jnp.where(j == 1, 4, j + 3)))
    return pl.pallas_call(
        functools.partial(_proj_gate_kernel, n_full=n_full, rem=rem, seg=seg, hpv=hpv, conv_k=conv_k),
        grid=(n_j, n_i),
        in_specs=[pl.BlockSpec((tm, d), lambda j, i: (i, 0)),
                  pl.BlockSpec((None, d, pw), w_map),
                  pl.BlockSpec((tm, cc), lambda j, i: (jnp.where(j >= 2, i, 0), pw // cc + cjob(j))),
                  pl.BlockSpec((None,) + cconv.shape[1:3] + (cc,), lambda j, i: (layer, 0, 0, cjob(j))),
                  pl.BlockSpec((None, conv_k, cc), lambda j, i: (layer, 0, cjob(j))),
                  pl.BlockSpec((None, 1, cc), lambda j, i: (layer, 0, cjob(j)))],
        out_specs=[pl.BlockSpec((tm, pw), lambda j, i: (i, j)),
                   pl.BlockSpec((tm, cc), lambda j, i: (jnp.where(j >= 2, i, 0), cjob(j)))],
        out_shape=[jax.ShapeDtypeStruct((n, 2 * pw + 2 * d), BF16),
                   jax.ShapeDtypeStruct((n, pw), F32)],
        scratch_shapes=[pltpu.VMEM((d, pw), BF16),
                        pltpu.VMEM((CONV_CHUNKS_PER_JOB, hpv + tm + n_seg * hpv, LANES), F32)],
        compiler_params=pltpu.CompilerParams(dimension_semantics=("arbitrary", "arbitrary"),
                                             vmem_limit_bytes=VMEM_LIMIT_BYTES),
        name="proj_gate",
    )(h, w_in, uv, cconv, cw, cb)


def _mixer_kernel(*refs, tm, n_p, seg, split_x, last, hpu):
    n_x = 2 if split_x else 1
    u_ref, cv_ref, sgp_ref, sgc_ref, smp_ref, smc_ref = refs[:6]
    x_refs = refs[6:6 + n_x]
    (cpool_ref, wmix_ref, wpo_ref, wco_ref, wout_ref, pscale_ref, lng_ref, lnb_ref,
     wn_ref) = refs[6 + n_x:15 + n_x]
    out_refs = refs[15 + n_x:17 + n_x]
    extu, pooled_scr = refs[17 + n_x:]
    n_chunks = u_ref.shape[1] // LANES
    grp_chunks = n_chunks // len(POOL_WINDOWS)
    i = pl.program_id(0)

    def pool_sum(c, w, row0, nrows):
        cur = extu[c, row0:row0 + nrows, :]
        s = cur
        for o in range(1, w):
            s = s + extu[c, row0 - o:row0 - o + nrows, :]
        return s, cur

    def pool_rows(base, nrows, out_row0, first_tile=None):
        head = 0 if first_tile is None else hpu
        if head:
            pos1 = lax.broadcasted_iota(jnp.int32, (head, LANES), 0) + 1
        for g, w in enumerate(POOL_WINDOWS):
            for cc in range(grp_chunks):
                c = g * grp_chunks + cc
                cols = slice(LANES * c, LANES * (c + 1))
                if head:
                    s, cur = pool_sum(c, w, base + hpu, head)
                    cnt = jnp.where(first_tile, jnp.minimum(w, pos1), w).astype(F32)
                    pooled_scr[out_row0:out_row0 + head, cols] = s / cnt - cur
                s, cur = pool_sum(c, w, base + hpu + head, nrows - head)
                pooled_scr[out_row0 + head:out_row0 + nrows, cols] = s * (1.0 / w) - cur

    def tail(x_ref, o_refs):
        pooled = pooled_scr[...]
        gw = pooled.shape[1] // len(POOL_WINDOWS)
        mixed = jnp.concatenate(
            [jnp.dot(pooled[:, gw * g:gw * (g + 1)].astype(BF16), wmix_ref[g], preferred_element_type=F32)
             for g in range(len(POOL_WINDOWS))], axis=1)
        pool_in = (mixed * pscale_ref[...] * sgp_ref[...].astype(F32)).astype(BF16)
        pool_branch = jnp.dot(pool_in, wpo_ref[...], preferred_element_type=F32)

        cv = cv_ref[...]
        mu = jnp.mean(cv, axis=-1, keepdims=True)
        xc = cv - mu
        var = jnp.mean(xc * xc, axis=-1, keepdims=True)
        ln = xc * lax.rsqrt(var + LN_EPS) * lng_ref[...] + lnb_ref[...]
        hl = ln.astype(BF16) * 0.5
        conv_in = (hl * jnp.tanh(hl) + hl) * sgc_ref[...]
        conv_branch = jnp.dot(conv_in, wco_ref[...], preferred_element_type=F32)

        smp, smc = smp_ref[...] * 0.5 + 0.5, smc_ref[...] * 0.5 + 0.5
        merged = smp * pool_branch.astype(BF16) + smc * conv_branch.astype(BF16)
        y = x_ref[...] + jnp.dot(merged, wout_ref[...], preferred_element_type=F32)
        ms = jnp.mean(y * y, axis=-1, keepdims=True)
        hn = y * lax.rsqrt(ms + RMS_EPS) * wn_ref[...]
        if last:
            o_refs[0][...] = hn
        else:
            o_refs[0][...] = y
            o_refs[1][...] = hn.astype(BF16)

    @pl.when(i < n_p)
    def _():
        @pl.when(i == 0)
        def _():
            extu[:, 0:hpu, :] = jnp.zeros((n_chunks, hpu, LANES), F32)

        @pl.when(i > 0)
        def _():
            extu[:, 0:hpu, :] = extu[:, tm:tm + hpu, :]

        for c in range(n_chunks):
            extu[c, hpu:hpu + tm, :] = u_ref[:, LANES * c:LANES * (c + 1)]
        pool_rows(0, tm, 0, first_tile=i == 0)
        tail(x_refs[0], out_refs if not last else out_refs[0:1])

    @pl.when(i >= n_p)
    def _():
        spt = tm // seg
        for s in range(spt):
            q = (i - n_p) * spt + s
            bu = s * (hpu + seg)
            for c in range(n_chunks):
                cols = slice(LANES * c, LANES * (c + 1))
                extu[c, bu + hpu - cpool_ref.shape[1]:bu + hpu, :] = cpool_ref[q, :, cols]
                extu[c, bu + hpu:bu + hpu + seg, :] = u_ref[seg * s:seg * (s + 1), cols]
            pool_rows(bu, seg, seg * s)
        tail(x_refs[-1], out_refs if not last else out_refs[1:2])


def _mixer_call(uv, cv, gates, xs, layer, cpool, wmix, wpo, wco, wout, pscale, lng, lnb, wn,
                *, tm, n_prompt_rows, seg, last):
    n = uv.shape[0]
    pw = uv.shape[1] // 2
    d = wout.shape[1]
    n_p = n_prompt_rows // tm
    n_s = (n - n_prompt_rows) // tm
    split_x = len(xs) == 2
    hpu = _round_up(cpool.shape[2], SUBLANES)
    spt = tm // seg
    assert PAST_LEN >= max(POOL_WINDOWS) - 1 and seg >= hpu and n_prompt_rows % tm == 0 and tm % seg == 0

    def const(a):
        rest = a.shape[1:]
        return pl.BlockSpec((None,) + rest, lambda i: (layer,) + (0,) * len(rest),
                            pipeline_mode=pl.Buffered(1))

    def whole(a):
        return pl.BlockSpec(a.shape, lambda i: (0,) * a.ndim, pipeline_mode=pl.Buffered(1))

    row = lambda c: (lambda i: (i, c))
    if split_x:
        x_specs = [pl.BlockSpec((tm, d), lambda i: (jnp.minimum(i, n_p - 1), 0)),
                   pl.BlockSpec((tm, d), lambda i: (jnp.maximum(i - n_p, 0), 0))]
    else:
        x_specs = [pl.BlockSpec((tm, d), row(0))]
    in_specs = ([pl.BlockSpec((tm, pw), row(0)), pl.BlockSpec((tm, pw), row(0)),
                 pl.BlockSpec((tm, pw), row(0)), pl.BlockSpec((tm, pw), row(1)),
                 pl.BlockSpec((tm, d), row(1)), pl.BlockSpec((tm, d), row(2))]
                + x_specs
                + [const(cpool)] + [whole(a) for a in (wmix, wpo, wco, wout)]
                + [const(a) for a in (pscale, lng, lnb, wn)])
    if last:
        out_specs = [pl.BlockSpec((tm, d), lambda i: (jnp.minimum(i, n_p - 1), 0)),
                     pl.BlockSpec((tm, d), lambda i: (jnp.maximum(i - n_p, 0), 0))]
        out_shape = [jax.ShapeDtypeStruct((n_prompt_rows, d), F32),
                     jax.ShapeDtypeStruct((n - n_prompt_rows, d), F32)]
    else:
        out_specs = [pl.BlockSpec((tm, d), row(0)), pl.BlockSpec((tm, d), row(0))]
        out_shape = [jax.ShapeDtypeStruct((n, d), F32), jax.ShapeDtypeStruct((n, d), BF16)]
    n_chunks = pw // LANES
    rows_u = max(hpu + tm, spt * (hpu + seg))
    kern = functools.partial(_mixer_kernel, tm=tm, n_p=n_p, seg=seg, split_x=split_x, last=last, hpu=hpu)
    return pl.pallas_call(
        kern,
        grid=(n_p + n_s,),
        in_specs=in_specs,
        out_specs=out_specs,
        out_shape=out_shape,
        scratch_shapes=[pltpu.VMEM((n_chunks, rows_u, LANES), F32),
                        pltpu.VMEM((tm, pw), F32)],
        compiler_params=pltpu.CompilerParams(dimension_semantics=("arbitrary",),
                                             vmem_limit_bytes=VMEM_LIMIT_BYTES),
        name="mixer_last" if last else "mixer",
    )(uv, cv, gates, gates, gates, gates, *xs, cpool, wmix, wpo, wco, wout, pscale, lng, lnb, wn)


def kernel(x_prompt, x_sample, cache_pool, cache_conv, w_norm, w_in, w_pool_mix, pool_scale, w_pool_out,
           conv_w, conv_b, ln_g, ln_b, w_conv_out, w_out, w_final_norm):
    b, t, d = x_prompt.shape
    sb, st, _ = x_sample.shape
    depth = w_norm.shape[0]
    pw = pool_scale.shape[1]
    hu, hv = cache_pool.shape[2], cache_conv.shape[2]
    assert b == 1 and conv_w.shape[2] == pw and 2 * pw == d and PROJ_TN == pw
    assert max(POOL_WINDOWS) - 1 == hu and conv_w.shape[1] - 1 == hv
    n_prompt = b * t
    xp = x_prompt.reshape(n_prompt, d)
    xs = x_sample.reshape(sb * st, d)
    cpool, cconv = cache_pool, cache_conv
    row2 = lambda a: a.reshape(depth, 1, a.shape[-1])
    pscale, cb, lng, lnb = row2(pool_scale), row2(conv_b), row2(ln_g), row2(ln_b)
    wnext = jnp.concatenate([w_norm[1:], w_final_norm[None]], axis=0).reshape(depth, 1, d)
    side_weights = (w_pool_mix, w_pool_out, w_conv_out, w_out)

    h = _norm_call(xp, xs, w_norm[0:1], tm=NORM_TM)
    x_parts = (xp, xs)
    new_states = []
    outs = None
    for l in range(depth):
        uv, states, (wmix, wpo, wco, wout) = _proj_uv_call(
            h, w_in, side_weights, l, tm=PROJ_UV_TM, pw=pw, n_prompt_rows=n_prompt, seg=st, hu=hu, hv=hv)
        new_states.append(states)
        gates, cv = _proj_gate_call(h, w_in, uv, cconv, conv_w, cb, l, tm=PROJ_GATE_TM, pw=pw,
                                    n_prompt_rows=n_prompt, seg=st)
        last = l == depth - 1
        outs = _mixer_call(uv, cv, gates, x_parts, l, cpool, wmix, wpo, wco, wout, pscale, lng, lnb, wnext,
                           tm=MIXER_TM, n_prompt_rows=n_prompt, seg=st, last=last)
        if not last:
            x_parts = (outs[0],)
            h = outs[1]
    y_prompt = outs[0].reshape(b, t, d)
    y_sample = outs[1].reshape(sb, st, d)
    return (y_prompt, y_sample) + tuple(jnp.stack(leaf) for leaf in zip(*new_states))
```

```python
import functools

import jax
import jax.numpy as jnp
from jax import lax
from jax.experimental import pallas as pl
from jax.experimental.pallas import tpu as pltpu

PAST_LEN = 2048
POOL_WINDOWS = (2, 4, 8, 16)
RMS_EPS = 1e-6
LN_EPS = 1e-5

LANES = 128
SUBLANES = 8
VMEM_LIMIT_BYTES = 60 * 1024 * 1024

NORM_TM = 512
PROJ_UV_TM = 1088
PROJ_GATE_TM = 1088
PROJ_TN = 1024
CONV_CHUNKS_PER_JOB = 2
CONV_ROWS = SUBLANES
MIXER_TM = 256

BF16 = jnp.bfloat16
F32 = jnp.float32


def _round_up(n, m):
    return (n + m - 1) // m * m


def _sigmoid(x):
    return 0.5 * jnp.tanh(0.5 * x) + 0.5


def _norm_kernel(xp_ref, xs_ref, wn_ref, h_ref, *, n_p):
    i = pl.program_id(0)

    def emit(x_ref):
        x = x_ref[...]
        ms = jnp.mean(x * x, axis=-1, keepdims=True)
        h_ref[...] = (x * lax.rsqrt(ms + RMS_EPS) * wn_ref[...]).astype(BF16)

    @pl.when(i < n_p)
    def _():
        emit(xp_ref)

    @pl.when(i >= n_p)
    def _():
        emit(xs_ref)


def _norm_call(xp, xs, wn, *, tm):
    n_p, n_s = xp.shape[0] // tm, xs.shape[0] // tm
    d = xp.shape[1]
    return pl.pallas_call(
        functools.partial(_norm_kernel, n_p=n_p),
        grid=(n_p + n_s,),
        in_specs=[pl.BlockSpec((tm, d), lambda i: (jnp.minimum(i, n_p - 1), 0)),
                  pl.BlockSpec((tm, d), lambda i: (jnp.maximum(i - n_p, 0), 0)),
                  pl.BlockSpec((1, d), lambda i: (0, 0))],
        out_specs=pl.BlockSpec((tm, d), lambda i: (i, 0)),
        out_shape=jax.ShapeDtypeStruct((xp.shape[0] + xs.shape[0], d), BF16),
        compiler_params=pltpu.CompilerParams(dimension_semantics=("arbitrary",)),
        name="rmsnorm_in",
    )(xp, xs, wn)


def _proj_uv_kernel(h_ref, wa_ref, wb_ref, *rest, n_side, rem, seg):
    side_in = rest[:n_side]
    uv_ref, pool_p_ref, conv_p_ref, pool_s_ref, conv_s_ref = rest[n_side:n_side + 5]
    side_out = rest[n_side + 5:2 * n_side + 5]
    wa_bf, wb_bf = rest[2 * n_side + 5:]
    j = pl.program_id(0)
    i = pl.program_id(1)
    last_tile = i == pl.num_programs(1) - 1

    def emit_states(p_ref, s_ref):
        hist = p_ref.shape[1]
        p_ref[0] = uv_ref[rem - hist:rem, :]
        for s in range(s_ref.shape[0]):
            end = rem + seg * (s + 1)
            s_ref[s] = uv_ref[end - hist:end, :]

    @pl.when(i == 0)
    def _():
        wa_bf[...] = wa_ref[...].astype(BF16)

    @pl.when((i == 0) & (j == 1))
    def _():
        wb_bf[...] = wb_ref[...].astype(BF16)

    for src, dst in zip(side_in, side_out):
        dst[...] = src[...].astype(BF16)

    def proj(w_bf):
        return jnp.dot(h_ref[...], w_bf[...], preferred_element_type=F32)

    @pl.when(j == 0)
    def _():
        uv_ref[...] = proj(wa_bf)

    @pl.when(j == 1)
    def _():
        uv_ref[...] = proj(wa_bf) * _sigmoid(proj(wb_bf))

    @pl.when((j == 0) & last_tile)
    def _():
        emit_states(pool_p_ref, pool_s_ref)

    @pl.when((j == 1) & last_tile)
    def _():
        emit_states(conv_p_ref, conv_s_ref)


def _proj_uv_call(h, w_in, side_weights, layer, *, tm, pw, n_prompt_rows, seg, hu, hv):
    n, d = h.shape
    n_i = n // tm
    steps = 2 * n_i
    n_full, rem = divmod(n_prompt_rows, tm)
    n_seg = (n - n_prompt_rows) // seg
    assert n_full + 1 == n_i and rem + n_seg * seg == tm and rem >= max(hu, hv) and seg >= max(hu, hv)
    state_shapes = [(1, hu, pw), (1, hv, pw), (n_seg, hu, pw), (n_seg, hv, pw)]
    side_specs, side_out_specs, side_shapes = [], [], []
    for w in side_weights:
        lead, (rows, cols) = w.shape[1:-2], w.shape[-2:]
        rb = max(rows // steps, 2 * SUBLANES)
        nblk = rows // rb
        assert rows % rb == 0 and steps % nblk == 0
        zeros = (0,) * len(lead)
        blk = lambda j, i, nblk=nblk: (j * n_i + i) * nblk // steps
        side_specs.append(pl.BlockSpec((None,) + lead + (rb, cols),
                                       lambda j, i, zeros=zeros, blk=blk: (layer,) + zeros + (blk(j, i), 0)))
        side_out_specs.append(pl.BlockSpec(lead + (rb, cols),
                                           lambda j, i, zeros=zeros, blk=blk: zeros + (blk(j, i), 0)))
        side_shapes.append(jax.ShapeDtypeStruct(lead + (rows, cols), BF16))
    outs = pl.pallas_call(
        functools.partial(_proj_uv_kernel, n_side=len(side_weights), rem=rem, seg=seg),
        grid=(2, n_i),
        in_specs=[pl.BlockSpec((tm, d), lambda j, i: (i, 0)),
                  pl.BlockSpec((None, d, pw), lambda j, i: (layer, 0, 2 * j),
                               pipeline_mode=pl.Buffered(1)),
                  pl.BlockSpec((None, d, pw), lambda j, i: (layer, 0, 3),
                               pipeline_mode=pl.Buffered(1))] + side_specs,
        out_specs=([pl.BlockSpec((tm, pw), lambda j, i: (i, j))]
                   + [pl.BlockSpec(sh, lambda j, i: (0, 0, 0)) for sh in state_shapes] + side_out_specs),
        out_shape=([jax.ShapeDtypeStruct((n, 2 * pw), F32)]
                   + [jax.ShapeDtypeStruct(sh, F32) for sh in state_shapes] + side_shapes),
        scratch_shapes=[pltpu.VMEM((d, pw), BF16), pltpu.VMEM((d, pw), BF16)],
        compiler_params=pltpu.CompilerParams(dimension_semantics=("arbitrary", "arbitrary"),
                                             vmem_limit_bytes=VMEM_LIMIT_BYTES),
        name="proj_uv",
    )(h, w_in, w_in, *side_weights)
    return outs[0], outs[1:5], outs[5:]


def _conv_block(ext_ref, c, start, wv, bv, prev, never):
    acc = bv if prev is None else jnp.where(never, prev, bv)
    for k, w in enumerate(wv):
        acc = acc + w * ext_ref[c, start + k:start + k + CONV_ROWS, :]
    return acc


def _proj_gate_kernel(h_ref, w_ref, v_ref, cconv_ref, cw_ref, cb_ref, g_ref, cv_ref, w_bf, ext,
                      *, n_full, rem, seg, hpv, conv_k):
    hv = cconv_ref.shape[1]
    j = pl.program_id(0)
    i = pl.program_id(1)
    tm = h_ref.shape[0]
    n_c = ext.shape[0]
    does_conv = j >= 2
    never = i < 0

    @pl.when(i == 0)
    def _():
        w_bf[...] = (w_ref[...] * 0.5).astype(BF16)

    @pl.when(does_conv & (i == 0))
    def _():
        ext[:, 0:hpv, :] = jnp.zeros((n_c, hpv, LANES), F32)

    @pl.when(does_conv & (i > 0))
    def _():
        ext[:, 0:hpv, :] = ext[:, tm:tm + hpv, :]

    def proj():
        return jnp.dot(h_ref[...], w_bf[...], preferred_element_type=F32)

    def conv_rows(c, acc, wv, bv, ext_row0, out_row0, nrows):
        cols = slice(LANES * c, LANES * (c + 1))
        for r in range(0, nrows, CONV_ROWS):
            acc = _conv_block(ext, c, ext_row0 - (conv_k - 1) + r, wv, bv, acc, never)
            cv_ref[out_row0 + r:out_row0 + r + CONV_ROWS, cols] = acc
        return acc

    def conv_tile(n_prompt_rows):
        n_seg = (tm - n_prompt_rows) // seg
        for c in range(n_c):
            cols = slice(LANES * c, LANES * (c + 1))
            wv = [jnp.broadcast_to(cw_ref[k:k + 1, cols], (CONV_ROWS, LANES)) for k in range(conv_k)]
            bv = jnp.broadcast_to(cb_ref[0:1, cols], (CONV_ROWS, LANES))
            acc = None
            if n_prompt_rows:
                ext[c, hpv:hpv + n_prompt_rows, :] = v_ref[0:n_prompt_rows, cols]
                acc = conv_rows(c, acc, wv, bv, hpv, 0, n_prompt_rows)
            for s in range(n_seg):
                base = hpv + n_prompt_rows + s * (hpv + seg)
                row0 = n_prompt_rows + seg * s
                ext[c, base + hpv - hv:base + hpv, :] = cconv_ref[s, :, cols]
                ext[c, base + hpv:base + hpv + seg, :] = v_ref[row0:row0 + seg, cols]
                acc = conv_rows(c, acc, wv, bv, base + hpv, row0, seg)

    @pl.when(j < 2)
    def _():
        hz = proj().astype(BF16)
        g_ref[...] = hz * jnp.tanh(hz) + hz

    @pl.when(does_conv & (i < n_full))
    def _():
        g_ref[...] = proj().astype(BF16)
        conv_tile(tm)

    @pl.when(does_conv & (i >= n_full))
    def _():
        g_ref[...] = proj().astype(BF16)
        conv_tile(rem)


def _proj_gate_call(h, w_in, uv, cconv, cw, cb, layer, *, tm, pw, n_prompt_rows, seg):
    n, d = h.shape
    n_i = n // tm
    n_full, rem = divmod(n_prompt_rows, tm)
    n_seg = (n - n_prompt_rows) // seg
    hpv = _round_up(cconv.shape[2], SUBLANES)
    conv_k = cw.shape[1]
    cc = CONV_CHUNKS_PER_JOB * LANES
    n_conv_jobs = pw // cc
    n_j = (2 * pw + 2 * d) // pw
    assert n_conv_jobs == n_j - 2 and n_full + 1 == n_i and rem + n_seg * seg == tm and n_seg == cconv.shape[1]
    assert rem % CONV_ROWS == 0 and seg % CONV_ROWS == 0 and tm % CONV_ROWS == 0 and seg >= hpv

    cjob = lambda j: jnp.maximum(j - 2, 0)
    w_map = lambda j, i: (layer, 0, jnp.where(j == 0, 1, jnp.where(j == 1, 4, j + 3)))
    return pl.pallas_call(
        functools.partial(_proj_gate_kernel, n_full=n_full, rem=rem, seg=seg, hpv=hpv, conv_k=conv_k),
        grid=(n_j, n_i),
        in_specs=[pl.BlockSpec((tm, d), lambda j, i: (i, 0)),
                  pl.BlockSpec((None, d, pw), w_map),
                  pl.BlockSpec((tm, cc), lambda j, i: (jnp.where(j >= 2, i, 0), pw // cc + cjob(j))),
                  pl.BlockSpec((None,) + cconv.shape[1:3] + (cc,), lambda j, i: (layer, 0, 0, cjob(j))),
                  pl.BlockSpec((None, conv_k, cc), lambda j, i: (layer, 0, cjob(j))),
                  pl.BlockSpec((None, 1, cc), lambda j, i: (layer, 0, cjob(j)))],
        out_specs=[pl.BlockSpec((tm, pw), lambda j, i: (i, j)),
                   pl.BlockSpec((tm, cc), lambda j, i: (jnp.where(j >= 2, i, 0), cjob(j)))],
        out_shape=[jax.ShapeDtypeStruct((n, 2 * pw + 2 * d), BF16),
                   jax.ShapeDtypeStruct((n, pw), F32)],
        scratch_shapes=[pltpu.VMEM((d, pw), BF16),
                        pltpu.VMEM((CONV_CHUNKS_PER_JOB, hpv + tm + n_seg * hpv, LANES), F32)],
        compiler_params=pltpu.CompilerParams(dimension_semantics=("arbitrary", "arbitrary"),
                                             vmem_limit_bytes=VMEM_LIMIT_BYTES),
        name="proj_gate",
    )(h, w_in, uv, cconv, cw, cb)


def _mixer_kernel(*refs, tm, n_p, seg, split_x, last, hpu):
    n_x = 2 if split_x else 1
    u_ref, cv_ref, sgp_ref, sgc_ref, smp_ref, smc_ref = refs[:6]
    x_refs = refs[6:6 + n_x]
    (cpool_ref, wmix_ref, wpo_ref, wco_ref, wout_ref, pscale_ref, lng_ref, lnb_ref,
     wn_ref) = refs[6 + n_x:15 + n_x]
    out_refs = refs[15 + n_x:17 + n_x]
    extu, pooled_scr = refs[17 + n_x:]
    n_chunks = u_ref.shape[1] // LANES
    grp_chunks = n_chunks // len(POOL_WINDOWS)
    i = pl.program_id(0)

    def pool_sum(c, w, row0, nrows):
        cur = extu[c, row0:row0 + nrows, :]
        s = cur
        for o in range(1, w):
            s = s + extu[c, row0 - o:row0 - o + nrows, :]
        return s, cur

    def pool_rows(base, nrows, out_row0, first_tile=None):
        head = 0 if first_tile is None else hpu
        if head:
            pos1 = lax.broadcasted_iota(jnp.int32, (head, LANES), 0) + 1
        for g, w in enumerate(POOL_WINDOWS):
            for cc in range(grp_chunks):
                c = g * grp_chunks + cc
                cols = slice(LANES * c, LANES * (c + 1))
                if head:
                    s, cur = pool_sum(c, w, base + hpu, head)
                    cnt = jnp.where(first_tile, jnp.minimum(w, pos1), w).astype(F32)
                    pooled_scr[out_row0:out_row0 + head, cols] = s / cnt - cur
                s, cur = pool_sum(c, w, base + hpu + head, nrows - head)
                pooled_scr[out_row0 + head:out_row0 + nrows, cols] = s * (1.0 / w) - cur

    def tail(x_ref, o_refs):
        pooled = pooled_scr[...]
        gw = pooled.shape[1] // len(POOL_WINDOWS)
        mixed = jnp.concatenate(
            [jnp.dot(pooled[:, gw * g:gw * (g + 1)].astype(BF16), wmix_ref[g], preferred_element_type=F32)
             for g in range(len(POOL_WINDOWS))], axis=1)
        pool_in = (mixed * pscale_ref[...] * sgp_ref[...].astype(F32)).astype(BF16)
        pool_branch = jnp.dot(pool_in, wpo_ref[...], preferred_element_type=F32)

        cv = cv_ref[...]
        mu = jnp.mean(cv, axis=-1, keepdims=True)
        xc = cv - mu
        var = jnp.mean(xc * xc, axis=-1, keepdims=True)
        ln = xc * lax.rsqrt(var + LN_EPS) * lng_ref[...] + lnb_ref[...]
        hl = ln.astype(BF16) * 0.5
        conv_in = (hl * jnp.tanh(hl) + hl) * sgc_ref[...]
        conv_branch = jnp.dot(conv_in, wco_ref[...], preferred_element_type=F32)

        smp, smc = jnp.tanh(smp_ref[...]) * 0.5 + 0.5, jnp.tanh(smc_ref[...]) * 0.5 + 0.5
        merged = smp * pool_branch.astype(BF16) + smc * conv_branch.astype(BF16)
        y = x_ref[...] + jnp.dot(merged, wout_ref[...], preferred_element_type=F32)
        ms = jnp.mean(y * y, axis=-1, keepdims=True)
        hn = y * lax.rsqrt(ms + RMS_EPS) * wn_ref[...]
        if last:
            o_refs[0][...] = hn
        else:
            o_refs[0][...] = y
            o_refs[1][...] = hn.astype(BF16)

    @pl.when(i < n_p)
    def _():
        @pl.when(i == 0)
        def _():
            extu[:, 0:hpu, :] = jnp.zeros((n_chunks, hpu, LANES), F32)

        @pl.when(i > 0)
        def _():
            extu[:, 0:hpu, :] = extu[:, tm:tm + hpu, :]

        for c in range(n_chunks):
            extu[c, hpu:hpu + tm, :] = u_ref[:, LANES * c:LANES * (c + 1)]
        pool_rows(0, tm, 0, first_tile=i == 0)
        tail(x_refs[0], out_refs if not last else out_refs[0:1])

    @pl.when(i >= n_p)
    def _():
        spt = tm // seg
        for s in range(spt):
            q = (i - n_p) * spt + s
            bu = s * (hpu + seg)
            for c in range(n_chunks):
                cols = slice(LANES * c, LANES * (c + 1))
                extu[c, bu + hpu - cpool_ref.shape[1]:bu + hpu, :] = cpool_ref[q, :, cols]
                extu[c, bu + hpu:bu + hpu + seg, :] = u_ref[seg * s:seg * (s + 1), cols]
            pool_rows(bu, seg, seg * s)
        tail(x_refs[-1], out_refs if not last else out_refs[1:2])


def _mixer_call(uv, cv, gates, xs, layer, cpool, wmix, wpo, wco, wout, pscale, lng, lnb, wn,
                *, tm, n_prompt_rows, seg, last):
    n = uv.shape[0]
    pw = uv.shape[1] // 2
    d = wout.shape[1]
    n_p = n_prompt_rows // tm
    n_s = (n - n_prompt_rows) // tm
    split_x = len(xs) == 2
    hpu = _round_up(cpool.shape[2], SUBLANES)
    spt = tm // seg
    assert PAST_LEN >= max(POOL_WINDOWS) - 1 and seg >= hpu and n_prompt_rows % tm == 0 and tm % seg == 0

    def const(a):
        rest = a.shape[1:]
        return pl.BlockSpec((None,) + rest, lambda i: (layer,) + (0,) * len(rest),
                            pipeline_mode=pl.Buffered(1))

    def whole(a):
        return pl.BlockSpec(a.shape, lambda i: (0,) * a.ndim, pipeline_mode=pl.Buffered(1))

    row = lambda c: (lambda i: (i, c))
    if split_x:
        x_specs = [pl.BlockSpec((tm, d), lambda i: (jnp.minimum(i, n_p - 1), 0)),
                   pl.BlockSpec((tm, d), lambda i: (jnp.maximum(i - n_p, 0), 0))]
    else:
        x_specs = [pl.BlockSpec((tm, d), row(0))]
    in_specs = ([pl.BlockSpec((tm, pw), row(0)), pl.BlockSpec((tm, pw), row(0)),
                 pl.BlockSpec((tm, pw), row(0)), pl.BlockSpec((tm, pw), row(1)),
                 pl.BlockSpec((tm, d), row(1)), pl.BlockSpec((tm, d), row(2))]
                + x_specs
                + [const(cpool)] + [whole(a) for a in (wmix, wpo, wco, wout)]
                + [const(a) for a in (pscale, lng, lnb, wn)])
    if last:
        out_specs = [pl.BlockSpec((tm, d), lambda i: (jnp.minimum(i, n_p - 1), 0)),
                     pl.BlockSpec((tm, d), lambda i: (jnp.maximum(i - n_p, 0), 0))]
        out_shape = [jax.ShapeDtypeStruct((n_prompt_rows, d), F32),
                     jax.ShapeDtypeStruct((n - n_prompt_rows, d), F32)]
    else:
        out_specs = [pl.BlockSpec((tm, d), row(0)), pl.BlockSpec((tm, d), row(0))]
        out_shape = [jax.ShapeDtypeStruct((n, d), F32), jax.ShapeDtypeStruct((n, d), BF16)]
    n_chunks = pw // LANES
    rows_u = max(hpu + tm, spt * (hpu + seg))
    kern = functools.partial(_mixer_kernel, tm=tm, n_p=n_p, seg=seg, split_x=split_x, last=last, hpu=hpu)
    return pl.pallas_call(
        kern,
        grid=(n_p + n_s,),
        in_specs=in_specs,
        out_specs=out_specs,
        out_shape=out_shape,
        scratch_shapes=[pltpu.VMEM((n_chunks, rows_u, LANES), F32),
                        pltpu.VMEM((tm, pw), F32)],
        compiler_params=pltpu.CompilerParams(dimension_semantics=("arbitrary",),
                                             vmem_limit_bytes=VMEM_LIMIT_BYTES),
        name="mixer_last" if last else "mixer",
    )(uv, cv, gates, gates, gates, gates, *xs, cpool, wmix, wpo, wco, wout, pscale, lng, lnb, wn)


def kernel(x_prompt, x_sample, cache_pool, cache_conv, w_norm, w_in, w_pool_mix, pool_scale, w_pool_out,
           conv_w, conv_b, ln_g, ln_b, w_conv_out, w_out, w_final_norm):
    b, t, d = x_prompt.shape
    sb, st, _ = x_sample.shape
    depth = w_norm.shape[0]
    pw = pool_scale.shape[1]
    hu, hv = cache_pool.shape[2], cache_conv.shape[2]
    assert b == 1 and conv_w.shape[2] == pw and 2 * pw == d and PROJ_TN == pw
    assert max(POOL_WINDOWS) - 1 == hu and conv_w.shape[1] - 1 == hv
    n_prompt = b * t
    xp = x_prompt.reshape(n_prompt, d)
    xs = x_sample.reshape(sb * st, d)
    cpool, cconv = cache_pool, cache_conv
    row2 = lambda a: a.reshape(depth, 1, a.shape[-1])
    pscale, cb, lng, lnb = row2(pool_scale), row2(conv_b), row2(ln_g), row2(ln_b)
    wnext = jnp.concatenate([w_norm[1:], w_final_norm[None]], axis=0).reshape(depth, 1, d)
    side_weights = (w_pool_mix, w_pool_out, w_conv_out, w_out)

    h = _norm_call(xp, xs, w_norm[0:1], tm=NORM_TM)
    x_parts = (xp, xs)
    new_states = []
    outs = None
    for l in range(depth):
        uv, states, (wmix, wpo, wco, wout) = _proj_uv_call(
            h, w_in, side_weights, l, tm=PROJ_UV_TM, pw=pw, n_prompt_rows=n_prompt, seg=st, hu=hu, hv=hv)
        new_states.append(states)
        gates, cv = _proj_gate_call(h, w_in, uv, cconv, conv_w, cb, l, tm=PROJ_GATE_TM, pw=pw,
                                    n_prompt_rows=n_prompt, seg=st)
        last = l == depth - 1
        outs = _mixer_call(uv, cv, gates, x_parts, l, cpool, wmix, wpo, wco, wout, pscale, lng, lnb, wnext,
                           tm=MIXER_TM, n_prompt_rows=n_prompt, seg=st, last=last)
        if not last:
            x_parts = (outs[0],)
            h = outs[1]
    y_prompt = outs[0].reshape(b, t, d)
    y_sample = outs[1].reshape(sb, st, d)
    return (y_prompt, y_sample) + tuple(jnp.stack(leaf) for leaf in zip(*new_states))
```

```python
import functools

import jax
import jax.numpy as jnp
from jax import lax
from jax.experimental import pallas as pl
from jax.experimental.pallas import tpu as pltpu

PAST_LEN = 2048
POOL_WINDOWS = (2, 4, 8, 16)
RMS_EPS = 1e-6
LN_EPS = 1e-5

LANES = 128
SUBLANES = 8
VMEM_LIMIT_BYTES = 60 * 1024 * 1024

NORM_TM = 512
PROJ_UV_TM = 1088
PROJ_GATE_TM = 1088
PROJ_TN = 1024
CONV_CHUNKS_PER_JOB = 2
CONV_ROWS = SUBLANES
MIXER_TM = 256

BF16 = jnp.bfloat16
F32 = jnp.float32


def _round_up(n, m):
    return (n + m - 1) // m * m


def _sigmoid(x):
    return 0.5 * jnp.tanh(0.5 * x) + 0.5


def _norm_kernel(xp_ref, xs_ref, wn_ref, h_ref, *, n_p):
    i = pl.program_id(0)

    def emit(x_ref):
        x = x_ref[...]
        ms = jnp.mean(x * x, axis=-1, keepdims=True)
        h_ref[...] = (x * lax.rsqrt(ms + RMS_EPS) * wn_ref[...]).astype(BF16)

    @pl.when(i < n_p)
    def _():
        emit(xp_ref)

    @pl.when(i >= n_p)
    def _():
        emit(xs_ref)


def _norm_call(xp, xs, wn, *, tm):
    n_p, n_s = xp.shape[0] // tm, xs.shape[0] // tm
    d = xp.shape[1]
    return pl.pallas_call(
        functools.partial(_norm_kernel, n_p=n_p),
        grid=(n_p + n_s,),
        in_specs=[pl.BlockSpec((tm, d), lambda i: (jnp.minimum(i, n_p - 1), 0)),
                  pl.BlockSpec((tm, d), lambda i: (jnp.maximum(i - n_p, 0), 0)),
                  pl.BlockSpec((1, d), lambda i: (0, 0))],
        out_specs=pl.BlockSpec((tm, d), lambda i: (i, 0)),
        out_shape=jax.ShapeDtypeStruct((xp.shape[0] + xs.shape[0], d), BF16),
        compiler_params=pltpu.CompilerParams(dimension_semantics=("arbitrary",)),
        name="rmsnorm_in",
    )(xp, xs, wn)


def _proj_uv_kernel(h_ref, wa_ref, wb_ref, *rest, n_side, rem, seg):
    side_in = rest[:n_side]
    uv_ref, pool_p_ref, conv_p_ref, pool_s_ref, conv_s_ref = rest[n_side:n_side + 5]
    side_out = rest[n_side + 5:2 * n_side + 5]
    wa_bf, wb_bf = rest[2 * n_side + 5:]
    j = pl.program_id(0)
    i = pl.program_id(1)
    last_tile = i == pl.num_programs(1) - 1

    def emit_states(p_ref, s_ref):
        hist = p_ref.shape[1]
        p_ref[0] = uv_ref[rem - hist:rem, :]
        for s in range(s_ref.shape[0]):
            end = rem + seg * (s + 1)
            s_ref[s] = uv_ref[end - hist:end, :]

    @pl.when(i == 0)
    def _():
        wa_bf[...] = wa_ref[...].astype(BF16)

    @pl.when((i == 0) & (j == 1))
    def _():
        wb_bf[...] = wb_ref[...].astype(BF16)

    for src, dst in zip(side_in, side_out):
        dst[...] = src[...].astype(BF16)

    def proj(w_bf):
        return jnp.dot(h_ref[...], w_bf[...], preferred_element_type=F32)

    @pl.when(j == 0)
    def _():
        uv_ref[...] = proj(wa_bf)

    @pl.when(j == 1)
    def _():
        uv_ref[...] = proj(wa_bf) * _sigmoid(proj(wb_bf))

    @pl.when((j == 0) & last_tile)
    def _():
        emit_states(pool_p_ref, pool_s_ref)

    @pl.when((j == 1) & last_tile)
    def _():
        emit_states(conv_p_ref, conv_s_ref)


def _proj_uv_call(h, w_in, side_weights, layer, *, tm, pw, n_prompt_rows, seg, hu, hv):
    n, d = h.shape
    n_i = n // tm
    steps = 2 * n_i
    n_full, rem = divmod(n_prompt_rows, tm)
    n_seg = (n - n_prompt_rows) // seg
    assert n_full + 1 == n_i and rem + n_seg * seg == tm and rem >= max(hu, hv) and seg >= max(hu, hv)
    state_shapes = [(1, hu, pw), (1, hv, pw), (n_seg, hu, pw), (n_seg, hv, pw)]
    side_specs, side_out_specs, side_shapes = [], [], []
    for w in side_weights:
        lead, (rows, cols) = w.shape[1:-2], w.shape[-2:]
        rb = max(rows // steps, 2 * SUBLANES)
        nblk = rows // rb
        assert rows % rb == 0 and steps % nblk == 0
        zeros = (0,) * len(lead)
        blk = lambda j, i, nblk=nblk: (j * n_i + i) * nblk // steps
        side_specs.append(pl.BlockSpec((None,) + lead + (rb, cols),
                                       lambda j, i, zeros=zeros, blk=blk: (layer,) + zeros + (blk(j, i), 0)))
        side_out_specs.append(pl.BlockSpec(lead + (rb, cols),
                                           lambda j, i, zeros=zeros, blk=blk: zeros + (blk(j, i), 0)))
        side_shapes.append(jax.ShapeDtypeStruct(lead + (rows, cols), BF16))
    outs = pl.pallas_call(
        functools.partial(_proj_uv_kernel, n_side=len(side_weights), rem=rem, seg=seg),
        grid=(2, n_i),
        in_specs=[pl.BlockSpec((tm, d), lambda j, i: (i, 0)),
                  pl.BlockSpec((None, d, pw), lambda j, i: (layer, 0, 2 * j),
                               pipeline_mode=pl.Buffered(1)),
                  pl.BlockSpec((None, d, pw), lambda j, i: (layer, 0, 3),
                               pipeline_mode=pl.Buffered(1))] + side_specs,
        out_specs=([pl.BlockSpec((tm, pw), lambda j, i: (i, j))]
                   + [pl.BlockSpec(sh, lambda j, i: (0, 0, 0)) for sh in state_shapes] + side_out_specs),
        out_shape=([jax.ShapeDtypeStruct((n, 2 * pw), F32)]
                   + [jax.ShapeDtypeStruct(sh, F32) for sh in state_shapes] + side_shapes),
        scratch_shapes=[pltpu.VMEM((d, pw), BF16), pltpu.VMEM((d, pw), BF16)],
        compiler_params=pltpu.CompilerParams(dimension_semantics=("arbitrary", "arbitrary"),
                                             vmem_limit_bytes=VMEM_LIMIT_BYTES),
        name="proj_uv",
    )(h, w_in, w_in, *side_weights)
    return outs[0], outs[1:5], outs[5:]


def _conv_block(ext_ref, c, start, wv, bv, prev, never):
    acc = bv if prev is None else jnp.where(never, prev, bv)
    for k, w in enumerate(wv):
        acc = acc + w * ext_ref[c, start + k:start + k + CONV_ROWS, :]
    return acc


def _proj_gate_kernel(h_ref, w_ref, v_ref, cconv_ref, cw_ref, cb_ref, g_ref, cv_ref, w_bf, ext,
                      *, n_full, rem, seg, hpv, conv_k):
    hv = cconv_ref.shape[1]
    j = pl.program_id(0)
    i = pl.program_id(1)
    tm = h_ref.shape[0]
    n_c = ext.shape[0]
    does_conv = j >= 2
    never = i < 0

    @pl.when(i == 0)
    def _():
        w_bf[...] = (w_ref[...] * 0.5).astype(BF16)

    @pl.when(does_conv & (i == 0))
    def _():
        ext[:, 0:hpv, :] = jnp.zeros((n_c, hpv, LANES), F32)

    @pl.when(does_conv & (i > 0))
    def _():
        ext[:, 0:hpv, :] = ext[:, tm:tm + hpv, :]

    def proj():
        return jnp.dot(h_ref[...], w_bf[...], preferred_element_type=F32)

    def conv_rows(c, acc, wv, bv, ext_row0, out_row0, nrows):
        cols = slice(LANES * c, LANES * (c + 1))
        for r in range(0, nrows, CONV_ROWS):
            acc = _conv_block(ext, c, ext_row0 - (conv_k - 1) + r, wv, bv, acc, never)
            cv_ref[out_row0 + r:out_row0 + r + CONV_ROWS, cols] = acc
        return acc

    def conv_tile(n_prompt_rows):
        n_seg = (tm - n_prompt_rows) // seg
        for c in range(n_c):
            cols = slice(LANES * c, LANES * (c + 1))
            wv = [jnp.broadcast_to(cw_ref[k:k + 1, cols], (CONV_ROWS, LANES)) for k in range(conv_k)]
            bv = jnp.broadcast_to(cb_ref[0:1, cols], (CONV_ROWS, LANES))
            acc = None
            if n_prompt_rows:
                ext[c, hpv:hpv + n_prompt_rows, :] = v_ref[0:n_prompt_rows, cols]
                acc = conv_rows(c, acc, wv, bv, hpv, 0, n_prompt_rows)
            for s in range(n_seg):
                base = hpv + n_prompt_rows + s * (hpv + seg)
                row0 = n_prompt_rows + seg * s
                ext[c, base + hpv - hv:base + hpv, :] = cconv_ref[s, :, cols]
                ext[c, base + hpv:base + hpv + seg, :] = v_ref[row0:row0 + seg, cols]
                acc = conv_rows(c, acc, wv, bv, base + hpv, row0, seg)

    @pl.when(j < 2)
    def _():
        hz = proj().astype(BF16)
        g_ref[...] = hz * jnp.tanh(hz) + hz

    @pl.when(does_conv & (i < n_full))
    def _():
        g_ref[...] = jnp.tanh(proj().astype(BF16))
        conv_tile(tm)

    @pl.when(does_conv & (i >= n_full))
    def _():
        g_ref[...] = jnp.tanh(proj().astype(BF16))
        conv_tile(rem)


def _proj_gate_call(h, w_in, uv, cconv, cw, cb, layer, *, tm, pw, n_prompt_rows, seg):
    n, d = h.shape
    n_i = n // tm
    n_full, rem = divmod(n_prompt_rows, tm)
    n_seg = (n - n_prompt_rows) // seg
    hpv = _round_up(cconv.shape[2], SUBLANES)
    conv_k = cw.shape[1]
    cc = CONV_CHUNKS_PER_JOB * LANES
    n_conv_jobs = pw // cc
    n_j = (2 * pw + 2 * d) // pw
    assert n_conv_jobs == n_j - 2 and n_full + 1 == n_i and rem + n_seg * seg == tm and n_seg == cconv.shape[1]
    assert rem % CONV_ROWS == 0 and seg % CONV_ROWS == 0 and tm % CONV_ROWS == 0 and seg >= hpv

    cjob = lambda j: jnp.maximum(j - 2, 0)
    w_map = lambda j, i: (layer, 0, jnp.where(j == 0, 1, jnp.where(j == 1, 4, j + 3)))
    return pl.pallas_call(
        functools.partial(_proj_gate_kernel, n_full=n_full, rem=rem, seg=seg, hpv=hpv, conv_k=conv_k),
        grid=(n_j, n_i),
        in_specs=[pl.BlockSpec((tm, d), lambda j, i: (i, 0)),
                  pl.BlockSpec((None, d, pw), w_map),
                  pl.BlockSpec((tm, cc), lambda j, i: (jnp.where(j >= 2, i, 0), pw // cc + cjob(j))),
                  pl.BlockSpec((None,) + cconv.shape[1:3] + (cc,), lambda j, i: (layer, 0, 0, cjob(j))),
                  pl.BlockSpec((None, conv_k, cc), lambda j, i: (layer, 0, cjob(j))),
                  pl.BlockSpec((None, 1, cc), lambda j, i: (layer, 0, cjob(j)))],
        out_specs=[pl.BlockSpec((tm, pw), lambda j, i: (i, j)),
                   pl.BlockSpec((tm, cc), lambda j, i: (jnp.where(j >= 2, i, 0), cjob(j)))],
        out_shape=[jax.ShapeDtypeStruct((n, 2 * pw + 2 * d), BF16),
                   jax.ShapeDtypeStruct((n, pw), F32)],
        scratch_shapes=[pltpu.VMEM((d, pw), BF16),
                        pltpu.VMEM((CONV_CHUNKS_PER_JOB, hpv + tm + n_seg * hpv, LANES), F32)],
        compiler_params=pltpu.CompilerParams(dimension_semantics=("arbitrary", "arbitrary"),
                                             vmem_limit_bytes=VMEM_LIMIT_BYTES),
        name="proj_gate",
    )(h, w_in, uv, cconv, cw, cb)


def _mixer_kernel(*refs, tm, n_p, seg, split_x, last, hpu):
    n_x = 2 if split_x else 1
    u_ref, cv_ref, sgp_ref, sgc_ref, smp_ref, smc_ref = refs[:6]
    x_refs = refs[6:6 + n_x]
    (cpool_ref, wmix_ref, wpo_ref, wco_ref, wout_ref, pscale_ref, lng_ref, lnb_ref,
     wn_ref) = refs[6 + n_x:15 + n_x]
    out_refs = refs[15 + n_x:17 + n_x]
    extu, pooled_scr = refs[17 + n_x:]
    n_chunks = u_ref.shape[1] // LANES
    grp_chunks = n_chunks // len(POOL_WINDOWS)
    i = pl.program_id(0)

    def pool_sum(c, w, row0, nrows):
        cur = extu[c, row0:row0 + nrows, :]
        s = cur
        for o in range(1, w):
            s = s + extu[c, row0 - o:row0 - o + nrows, :]
        return s, cur

    def pool_rows(base, nrows, out_row0, first_tile=None):
        head = 0 if first_tile is None else hpu
        if head:
            pos1 = lax.broadcasted_iota(jnp.int32, (head, LANES), 0) + 1
        for g, w in enumerate(POOL_WINDOWS):
            for cc in range(grp_chunks):
                c = g * grp_chunks + cc
                cols = slice(LANES * c, LANES * (c + 1))
                if head:
                    s, cur = pool_sum(c, w, base + hpu, head)
                    cnt = jnp.where(first_tile, jnp.minimum(w, pos1), w).astype(F32)
                    pooled_scr[out_row0:out_row0 + head, cols] = s / cnt - cur
                s, cur = pool_sum(c, w, base + hpu + head, nrows - head)
                pooled_scr[out_row0 + head:out_row0 + nrows, cols] = s * (1.0 / w) - cur

    def tail(x_ref, o_refs):
        pooled = pooled_scr[...]
        gw = pooled.shape[1] // len(POOL_WINDOWS)
        mixed = jnp.concatenate(
            [jnp.dot(pooled[:, gw * g:gw * (g + 1)].astype(BF16), wmix_ref[g], preferred_element_type=F32)
             for g in range(len(POOL_WINDOWS))], axis=1)
        pool_in = (mixed * pscale_ref[...] * sgp_ref[...].astype(F32)).astype(BF16)
        pool_branch = jnp.dot(pool_in, wpo_ref[...], preferred_element_type=F32)

        cv = cv_ref[...]
        mu = jnp.mean(cv, axis=-1, keepdims=True)
        xc = cv - mu
        var = jnp.mean(xc * xc, axis=-1, keepdims=True)
        ln = xc * lax.rsqrt(var + LN_EPS) * lng_ref[...] + lnb_ref[...]
        hl = ln.astype(BF16) * 0.5
        conv_in = (hl * jnp.tanh(hl) + hl) * sgc_ref[...]
        conv_branch = jnp.dot(conv_in, wco_ref[...], preferred_element_type=F32)

        smp, smc = smp_ref[...] * 0.5 + 0.5, smc_ref[...] * 0.5 + 0.5
        merged = smp * pool_branch.astype(BF16) + smc * conv_branch.astype(BF16)
        y = x_ref[...] + jnp.dot(merged, wout_ref[...], preferred_element_type=F32)
        ms = jnp.mean(y * y, axis=-1, keepdims=True)
        hn = y * lax.rsqrt(ms + RMS_EPS) * wn_ref[...]
        if last:
            o_refs[0][...] = hn
        else:
            o_refs[0][...] = y
            o_refs[1][...] = hn.astype(BF16)

    @pl.when(i < n_p)
    def _():
        @pl.when(i == 0)
        def _():
            extu[:, 0:hpu, :] = jnp.zeros((n_chunks, hpu, LANES), F32)

        @pl.when(i > 0)
        def _():
            extu[:, 0:hpu, :] = extu[:, tm:tm + hpu, :]

        for c in range(n_chunks):
            extu[c, hpu:hpu + tm, :] = u_ref[:, LANES * c:LANES * (c + 1)]
        pool_rows(0, tm, 0, first_tile=i == 0)
        tail(x_refs[0], out_refs if not last else out_refs[0:1])

    @pl.when(i >= n_p)
    def _():
        spt = tm // seg
        for s in range(spt):
            q = (i - n_p) * spt + s
            bu = s * (hpu + seg)
            for c in range(n_chunks):
                cols = slice(LANES * c, LANES * (c + 1))
                extu[c, bu + hpu - cpool_ref.shape[1]:bu + hpu, :] = cpool_ref[q, :, cols]
                extu[c, bu + hpu:bu + hpu + seg, :] = u_ref[seg * s:seg * (s + 1), cols]
            pool_rows(bu, seg, seg * s)
        tail(x_refs[-1], out_refs if not last else out_refs[1:2])


def _mixer_call(uv, cv, gates, xs, layer, cpool, wmix, wpo, wco, wout, pscale, lng, lnb, wn,
                *, tm, n_prompt_rows, seg, last):
    n = uv.shape[0]
    pw = uv.shape[1] // 2
    d = wout.shape[1]
    n_p = n_prompt_rows // tm
    n_s = (n - n_prompt_rows) // tm
    split_x = len(xs) == 2
    hpu = _round_up(cpool.shape[2], SUBLANES)
    spt = tm // seg
    assert PAST_LEN >= max(POOL_WINDOWS) - 1 and seg >= hpu and n_prompt_rows % tm == 0 and tm % seg == 0

    def const(a):
        rest = a.shape[1:]
        return pl.BlockSpec((None,) + rest, lambda i: (layer,) + (0,) * len(rest),
                            pipeline_mode=pl.Buffered(1))

    def whole(a):
        return pl.BlockSpec(a.shape, lambda i: (0,) * a.ndim, pipeline_mode=pl.Buffered(1))

    row = lambda c: (lambda i: (i, c))
    if split_x:
        x_specs = [pl.BlockSpec((tm, d), lambda i: (jnp.minimum(i, n_p - 1), 0)),
                   pl.BlockSpec((tm, d), lambda i: (jnp.maximum(i - n_p, 0), 0))]
    else:
        x_specs = [pl.BlockSpec((tm, d), row(0))]
    in_specs = ([pl.BlockSpec((tm, pw), row(0)), pl.BlockSpec((tm, pw), row(0)),
                 pl.BlockSpec((tm, pw), row(0)), pl.BlockSpec((tm, pw), row(1)),
                 pl.BlockSpec((tm, d), row(1)), pl.BlockSpec((tm, d), row(2))]
                + x_specs
                + [const(cpool)] + [whole(a) for a in (wmix, wpo, wco, wout)]
                + [const(a) for a in (pscale, lng, lnb, wn)])
    if last:
        out_specs = [pl.BlockSpec((tm, d), lambda i: (jnp.minimum(i, n_p - 1), 0)),
                     pl.BlockSpec((tm, d), lambda i: (jnp.maximum(i - n_p, 0), 0))]
        out_shape = [jax.ShapeDtypeStruct((n_prompt_rows, d), F32),
                     jax.ShapeDtypeStruct((n - n_prompt_rows, d), F32)]
    else:
        out_specs = [pl.BlockSpec((tm, d), row(0)), pl.BlockSpec((tm, d), row(0))]
        out_shape = [jax.ShapeDtypeStruct((n, d), F32), jax.ShapeDtypeStruct((n, d), BF16)]
    n_chunks = pw // LANES
    rows_u = max(hpu + tm, spt * (hpu + seg))
    kern = functools.partial(_mixer_kernel, tm=tm, n_p=n_p, seg=seg, split_x=split_x, last=last, hpu=hpu)
    return pl.pallas_call(
        kern,
        grid=(n_p + n_s,),
        in_specs=in_specs,
        out_specs=out_specs,
        out_shape=out_shape,
        scratch_shapes=[pltpu.VMEM((n_chunks, rows_u, LANES), F32),
                        pltpu.VMEM((tm, pw), F32)],
        compiler_params=pltpu.CompilerParams(dimension_semantics=("arbitrary",),
                                             vmem_limit_bytes=VMEM_LIMIT_BYTES),
        name="mixer_last" if last else "mixer",
    )(uv, cv, gates, gates, gates, gates, *xs, cpool, wmix, wpo, wco, wout, pscale, lng, lnb, wn)


def kernel(x_prompt, x_sample, cache_pool, cache_conv, w_norm, w_in, w_pool_mix, pool_scale, w_pool_out,
           conv_w, conv_b, ln_g, ln_b, w_conv_out, w_out, w_final_norm):
    b, t, d = x_prompt.shape
    sb, st, _ = x_sample.shape
    depth = w_norm.shape[0]
    pw = pool_scale.shape[1]
    hu, hv = cache_pool.shape[2], cache_conv.shape[2]
    assert b == 1 and conv_w.shape[2] == pw and 2 * pw == d and PROJ_TN == pw
    assert max(POOL_WINDOWS) - 1 == hu and conv_w.shape[1] - 1 == hv
    n_prompt = b * t
    xp = x_prompt.reshape(n_prompt, d)
    xs = x_sample.reshape(sb * st, d)
    cpool, cconv = cache_pool, cache_conv
    row2 = lambda a: a.reshape(depth, 1, a.shape[-1])
    pscale, cb, lng, lnb = row2(pool_scale), row2(conv_b), row2(ln_g), row2(ln_b)
    wnext = jnp.concatenate([w_norm[1:], w_final_norm[None]], axis=0).reshape(depth, 1, d)
    side_weights = (w_pool_mix, w_pool_out, w_conv_out, w_out)

    h = _norm_call(xp, xs, w_norm[0:1], tm=NORM_TM)
    x_parts = (xp, xs)
    new_states = []
    outs = None
    for l in range(depth):
        uv, states, (wmix, wpo, wco, wout) = _proj_uv_call(
            h, w_in, side_weights, l, tm=PROJ_UV_TM, pw=pw, n_prompt_rows=n_prompt, seg=st, hu=hu, hv=hv)
        new_states.append(states)
        gates, cv = _proj_gate_call(h, w_in, uv, cconv, conv_w, cb, l, tm=PROJ_GATE_TM, pw=pw,
                                    n_prompt_rows=n_prompt, seg=st)
        last = l == depth - 1
        outs = _mixer_call(uv, cv, gates, x_parts, l, cpool, wmix, wpo, wco, wout, pscale, lng, lnb, wnext,
                           tm=MIXER_TM, n_prompt_rows=n_prompt, seg=st, last=last)
        if not last:
            x_parts = (outs[0],)
            h = outs[1]
    y_prompt = outs[0].reshape(b, t, d)
    y_sample = outs[1].reshape(sb, st, d)
    return (y_prompt, y_sample) + tuple(jnp.stack(leaf) for leaf in zip(*new_states))
```
